```python
import jax, jax.numpy as jnp
from jax import lax
import numpy as np

D_MODEL = 1024
BATCH = 1
SEQ = 16384
DEPTH = 4

GRID_W = 64
CTX_LEN = 256
HEAD_DIM = 64
EPS = 1e-6
A_HEADS = 8
A_KV_HEADS = 2
A_GROUP = A_HEADS // A_KV_HEADS
A_Q = A_HEADS * HEAD_DIM
A_KV = A_KV_HEADS * HEAD_DIM
A_IN = A_Q + 2 * A_KV
Q_BLOCK = 128
ROPE_THETA = 10000.0
ROPE_FREQS = HEAD_DIM // 4
B_GROUPS = 8
B_CH = 64
B_WIDTH = B_GROUPS * B_CH
CHUNK = 128
C_HEADS = 8
C_W = C_HEADS * HEAD_DIM
NA_MAX_ROWS = 8
NA_COLS = 16
D_CH = 512
CONV_W = 31
MIX_WIDTH = A_Q + B_WIDTH
EVEN_IN = A_IN + 2 * B_WIDTH
ODD_IN = 3 * C_W + 2 * D_CH
N_EVEN = (DEPTH + 1) // 2
N_ODD = DEPTH // 2
N_EXPERTS = 16
N_GROUPS = 4
EXPERTS_PER_GROUP = N_EXPERTS // N_GROUPS
TOPK_GROUPS = 1
TOP_K = 2
D_EXPERT = 512
MOE_BLOCK = 256

kernel_name = 'hybrid_flow_backbone'


def rms_norm(x, g):
    xf = x.astype(jnp.float32)
    y = xf * lax.rsqrt(jnp.mean(xf * xf, axis=-1, keepdims=True) + EPS)
    return (y * g.astype(jnp.float32)).astype(x.dtype)


def layer_norm(x, g, b):
    xf = x.astype(jnp.float32)
    mu = jnp.mean(xf, axis=-1, keepdims=True)
    var = jnp.mean(jnp.square(xf - mu), axis=-1, keepdims=True)
    y = (xf - mu) * lax.rsqrt(var + EPS)
    return (y * g.astype(jnp.float32) + b.astype(jnp.float32)).astype(x.dtype)


def axial_rope_tables(n_tok, dtype):
    t = jnp.arange(n_tok, dtype=jnp.int32)
    pos = jnp.stack([t // GRID_W, t % GRID_W], axis=-1).astype(jnp.float32)
    inv = ROPE_THETA ** (-jnp.arange(ROPE_FREQS, dtype=jnp.float32) / ROPE_FREQS)
    ang = pos[:, :, None] * inv
    return jnp.cos(ang).astype(dtype), jnp.sin(ang).astype(dtype)


def apply_axial_rope(x, cos, sin):
    b, s, h, _ = x.shape
    xs = x.reshape(b, s, h, 2, 2, ROPE_FREQS)
    x1, x2 = xs[..., 0, :], xs[..., 1, :]
    cs, sn = cos[None, :, None], sin[None, :, None]
    out = jnp.stack([x1 * cs - x2 * sn, x2 * cs + x1 * sn], axis=-2)
    return out.reshape(x.shape)


def dense_gqa(q, k, v):
    s = jnp.einsum('bqkgd,bskd->bkgqs', q, k, preferred_element_type=jnp.float32) * (HEAD_DIM ** -0.5)
    p = jax.nn.softmax(s, axis=-1).astype(v.dtype)
    return jnp.einsum('bkgqs,bskd->bqkgd', p, v)


def blocked_gqa(q, k, v):
    b, s = q.shape[:2]
    nb = s // Q_BLOCK
    qb = jnp.moveaxis(q.reshape(b, nb, Q_BLOCK, *q.shape[2:]), 1, 0)
    o = lax.map(lambda qi: dense_gqa(qi, k, v), qb)
    return jnp.moveaxis(o, 0, 1).reshape(q.shape)


def mixer_a(p_lat, p_ctx, q_g, k_g, need_ctx):
    b, s, _ = p_lat.shape

    def heads(p):
        lead = p.shape[:2]
        q = rms_norm(p[..., :A_Q].reshape(*lead, A_HEADS, HEAD_DIM), q_g)
        k = rms_norm(p[..., A_Q:A_Q + A_KV].reshape(*lead, A_KV_HEADS, HEAD_DIM), k_g)
        v = p[..., A_Q + A_KV:A_IN].reshape(*lead, A_KV_HEADS, HEAD_DIM)
        return q, k, v

    ql, kl, vl = heads(p_lat)
    qc, kc, vc = heads(p_ctx)
    cos, sin = axial_rope_tables(s, ql.dtype)
    ql = apply_axial_rope(ql, cos, sin)
    kl = apply_axial_rope(kl, cos, sin)
    k_all = jnp.concatenate([kc, kl], axis=1)
    v_all = jnp.concatenate([vc, vl], axis=1)
    o_lat = blocked_gqa(ql.reshape(b, s, A_KV_HEADS, A_GROUP, HEAD_DIM), k_all, v_all).reshape(b, s, A_Q)
    if not need_ctx:
        return o_lat, None
    o_ctx = dense_gqa(qc.reshape(b, -1, A_KV_HEADS, A_GROUP, HEAD_DIM), kc, vc).reshape(b, -1, A_Q)
    return o_lat, o_ctx


def chunk_spatial_gate(p, v_g, ws, bs):
    b, n_tok, _ = p.shape
    n = n_tok // CHUNK
    z = jax.nn.gelu(p)
    u, v = z[..., :B_WIDTH], z[..., B_WIDTH:]
    v = rms_norm(v.reshape(b, n, CHUNK, B_GROUPS, B_CH), v_g.reshape(B_GROUPS, B_CH))
    mixed = jnp.einsum('gts,bnsgc->bntgc', ws, v) + bs.T[:, :, None]
    return u * mixed.reshape(b, n_tok, B_WIDTH)


def neighbourhood_attention(q, k, v, k_ctx, v_ctx, rpb):
    b, s, h, dh = q.shape
    rows = s // GRID_W
    wh = min(NA_MAX_ROWS, rows)
    n_loc = wh * NA_COLS
    qg, kg, vg = (t.reshape(b, rows, GRID_W, h, dh) for t in (q, k, v))
    r = jnp.arange(rows, dtype=jnp.int32)
    row_idx = jnp.clip(r - wh // 2, 0, rows - wh)[:, None] + jnp.arange(wh, dtype=jnp.int32)[None, :]
    cc = jnp.arange(GRID_W, dtype=jnp.int32)
    col_idx = jnp.clip(cc - NA_COLS // 2, 0, GRID_W - NA_COLS)[:, None] + jnp.arange(NA_COLS, dtype=jnp.int32)[None, :]
    rpb_cols = rpb.astype(jnp.float32)[:, :, col_idx - cc[:, None] + NA_COLS - 1]
    scale = HEAD_DIM ** -0.5

    def row_block(args):
        qr, ridx, ri = args
        kr = jnp.take(kg, ridx, axis=1)[:, :, col_idx]
        vr = jnp.take(vg, ridx, axis=1)[:, :, col_idx]
        bias = jnp.transpose(rpb_cols[:, ridx - ri + NA_MAX_ROWS - 1], (0, 2, 1, 3))
        s_loc = jnp.einsum('bchd,bicjhd->bhcij', qr, kr, preferred_element_type=jnp.float32) * scale + bias
        s_ctx = jnp.einsum('bchd,bshd->bhcs', qr, k_ctx, preferred_element_type=jnp.float32) * scale
        sc = jnp.concatenate([s_loc.reshape(b, h, GRID_W, n_loc), s_ctx], axis=-1)
        p = jax.nn.softmax(sc, axis=-1).astype(v.dtype)
        p_loc = p[..., :n_loc].reshape(b, h, GRID_W, wh, NA_COLS)
        return jnp.einsum('bhcij,bicjhd->bchd', p_loc, vr) + jnp.einsum('bhcs,bshd->bchd', p[..., n_loc:], v_ctx)

    o = lax.map(row_block, (jnp.moveaxis(qg, 1, 0), row_idx, r))
    return jnp.moveaxis(o, 0, 1).reshape(b, s, h * dh)


def mixer_c(p_lat, p_ctx, q_g, k_g, rpb, need_ctx):
    def heads(p):
        lead = p.shape[:2]
        q = rms_norm(p[..., :C_W].reshape(*lead, C_HEADS, HEAD_DIM), q_g)
        k = rms_norm(p[..., C_W:2 * C_W].reshape(*lead, C_HEADS, HEAD_DIM), k_g)
        v = p[..., 2 * C_W:3 * C_W].reshape(*lead, C_HEADS, HEAD_DIM)
        return q, k, v

    ql, kl, vl = heads(p_lat)
    qc, kc, vc = heads(p_ctx)
    o_lat = neighbourhood_attention(ql, kl, vl, kc, vc, rpb)
    if not need_ctx:
        return o_lat, None
    b = qc.shape[0]
    o_ctx = dense_gqa(qc[:, :, :, None], kc, vc).reshape(b, -1, C_W)
    return o_lat, o_ctx


def conformer_conv(p, dw, dw_b, ln_g, ln_b):
    a, g = p[..., :D_CH], p[..., D_CH:]
    y = a * jax.nn.sigmoid(g)
    y = lax.conv_general_dilated(y, dw[:, None, :], window_strides=(1,),
                                 padding=[(CONV_W // 2, CONV_W // 2)],
                                 dimension_numbers=('NWC', 'WIO', 'NWC'),
                                 feature_group_count=D_CH) + dw_b
    return jax.nn.silu(layer_norm(y, ln_g, ln_b))


def moe_ffn(h, router_w, router_b, w_gate, w_up, w_down):
    t, d = h.shape
    aff = jax.nn.sigmoid(jnp.dot(h, router_w, preferred_element_type=jnp.float32))
    sel = aff + router_b.astype(jnp.float32)
    gscore = lax.top_k(sel.reshape(t, N_GROUPS, EXPERTS_PER_GROUP), 2)[0].sum(-1)
    _, gtop = lax.top_k(gscore, TOPK_GROUPS)
    gmask = jnp.any(gtop[:, :, None] == jnp.arange(N_GROUPS)[None, None, :], axis=1)
    emask = jnp.repeat(gmask, EXPERTS_PER_GROUP, axis=1)
    _, eidx = lax.top_k(jnp.where(emask, sel, -jnp.inf), TOP_K)
    w = jnp.take_along_axis(aff, eidx, axis=1)
    w = w / jnp.sum(w, axis=-1, keepdims=True)
    n_assign = t * TOP_K
    e_flat = eidx.reshape(n_assign)
    tok = jnp.repeat(jnp.arange(t, dtype=jnp.int32), TOP_K)
    gate = w.reshape(n_assign).astype(h.dtype)
    order = jnp.argsort(e_flat)
    e_s, tok_s, gate_s = e_flat[order], tok[order], gate[order]
    counts = jnp.bincount(e_flat, length=N_EXPERTS)
    start = jnp.cumsum(counts) - counts
    pcounts = (counts + MOE_BLOCK - 1) // MOE_BLOCK * MOE_BLOCK
    pend = jnp.cumsum(pcounts)
    pstart = pend - pcounts
    dest = pstart[e_s] + jnp.arange(n_assign, dtype=jnp.int32) - start[e_s]
    n_blocks = -(-n_assign // MOE_BLOCK) + N_EXPERTS
    buf = jnp.zeros((n_blocks * MOE_BLOCK, d), h.dtype).at[dest].set(h[tok_s])
    blk_e = jnp.minimum(jnp.searchsorted(pend, jnp.arange(n_blocks, dtype=jnp.int32) * MOE_BLOCK, side='right'), N_EXPERTS - 1)

    def expert_block(args):
        xb, e = args
        return (jax.nn.silu(xb @ w_gate[e]) * (xb @ w_up[e])) @ w_down[e]

    y = lax.map(expert_block, (buf.reshape(n_blocks, MOE_BLOCK, d), blk_e)).reshape(-1, d)
    return jnp.zeros_like(h).at[tok_s].add(y[dest] * gate_s[:, None])


def setup_inputs(seed: int = 0) -> dict:
    key = jax.random.key(seed)
    ks = jax.random.split(key, 32)
    f32 = jnp.float32

    def nrm(k, shape, scale):
        return jax.random.normal(k, shape, f32) * scale

    def gain(k, shape):
        return 1.0 + nrm(k, shape, 0.05)

    return {
        'x': nrm(ks[0], (BATCH, SEQ, D_MODEL), 1.0),
        'c': nrm(ks[1], (BATCH, D_MODEL), 1.0),
        'ctx': nrm(ks[2], (BATCH, CTX_LEN, D_MODEL), 1.0),
        'c_ctx': nrm(ks[3], (D_MODEL,), 1.0),
        'router_w': nrm(ks[4], (D_MODEL, N_EXPERTS), D_MODEL ** -0.5),
        'router_b': nrm(ks[5], (N_EXPERTS,), 0.01),
        'ada_w': nrm(ks[6], (DEPTH, D_MODEL, 6 * D_MODEL), 0.5 * D_MODEL ** -0.5),
        'ada_b': nrm(ks[7], (DEPTH, 6 * D_MODEL), 0.02),
        'norm1_g': gain(ks[8], (DEPTH, D_MODEL)),
        'norm2_g': gain(ks[9], (DEPTH, D_MODEL)),
        'ev_w_in': nrm(ks[10], (N_EVEN, D_MODEL, EVEN_IN), D_MODEL ** -0.5),
        'ev_w_out': nrm(ks[11], (N_EVEN, MIX_WIDTH, D_MODEL), MIX_WIDTH ** -0.5),
        'a_q_norm': gain(ks[12], (N_EVEN, HEAD_DIM)),
        'a_k_norm': gain(ks[13], (N_EVEN, HEAD_DIM)),
        'b_v_norm': gain(ks[14], (N_EVEN, B_WIDTH)),
        'b_ws': nrm(ks[15], (N_EVEN, B_GROUPS, CHUNK, CHUNK), CHUNK ** -0.5),
        'b_bs': nrm(ks[16], (N_EVEN, B_GROUPS, CHUNK), 0.02),
        'od_w_in': nrm(ks[17], (N_ODD, D_MODEL, ODD_IN), D_MODEL ** -0.5),
        'od_w_out': nrm(ks[18], (N_ODD, MIX_WIDTH, D_MODEL), MIX_WIDTH ** -0.5),
        'c_q_norm': gain(ks[19], (N_ODD, HEAD_DIM)),
        'c_k_norm': gain(ks[20], (N_ODD, HEAD_DIM)),
        'c_rpb': nrm(ks[21], (N_ODD, C_HEADS, 2 * NA_MAX_ROWS - 1, 2 * NA_COLS - 1), 0.1),
        'd_dw': nrm(ks[22], (N_ODD, CONV_W, D_CH), CONV_W ** -0.5),
        'd_dw_b': nrm(ks[23], (N_ODD, D_CH), 0.02),
        'd_ln_g': gain(ks[24], (N_ODD, D_CH)),
        'd_ln_b': nrm(ks[25], (N_ODD, D_CH), 0.02),
        'moe_w_gate': nrm(ks[26], (DEPTH, N_EXPERTS, D_MODEL, D_EXPERT), D_MODEL ** -0.5),
        'moe_w_up': nrm(ks[27], (DEPTH, N_EXPERTS, D_MODEL, D_EXPERT), D_MODEL ** -0.5),
        'moe_w_down': nrm(ks[28], (DEPTH, N_EXPERTS, D_EXPERT, D_MODEL), D_EXPERT ** -0.5),
    }


def reference(x, c, ctx, c_ctx, router_w, router_b, ada_w, ada_b, norm1_g, norm2_g,
              ev_w_in, ev_w_out, a_q_norm, a_k_norm, b_v_norm, b_ws, b_bs,
              od_w_in, od_w_out, c_q_norm, c_k_norm, c_rpb, d_dw, d_dw_b, d_ln_g, d_ln_b,
              moe_w_gate, moe_w_up, moe_w_down):
    b, s, d = x.shape
    n_ctx = ctx.shape[1]
    x_lat, x_ctx = x, ctx
    for l in range(DEPTH):
        need_ctx = l < DEPTH - 1
        mod_lat = jax.nn.silu(c) @ ada_w[l] + ada_b[l]
        mod_ctx = jax.nn.silu(c_ctx) @ ada_w[l] + ada_b[l]
        sh1, sc1, g1, sh2, sc2, g2 = jnp.split(mod_lat[:, None, :], 6, axis=-1)
        ch1, cs1, cg1, ch2, cs2, cg2 = jnp.split(mod_ctx, 6)
        h_lat = rms_norm(x_lat, norm1_g[l]) * (1.0 + sc1) + sh1
        h_ctx = rms_norm(x_ctx, norm1_g[l]) * (1.0 + cs1) + ch1
        i = l // 2
        if l % 2 == 0:
            w_in, w_out = ev_w_in[i], ev_w_out[i]
            p_lat, p_ctx = h_lat @ w_in, h_ctx @ w_in
            a_lat, a_ctx = mixer_a(p_lat[..., :A_IN], p_ctx[..., :A_IN], a_q_norm[i], a_k_norm[i], need_ctx)
            b_lat = chunk_spatial_gate(p_lat[..., A_IN:], b_v_norm[i], b_ws[i], b_bs[i])
            m_lat = jnp.concatenate([a_lat, b_lat], axis=-1)
            if need_ctx:
                b_ctx = chunk_spatial_gate(p_ctx[..., A_IN:], b_v_norm[i], b_ws[i], b_bs[i])
                m_ctx = jnp.concatenate([a_ctx, b_ctx], axis=-1)
        else:
            w_in, w_out = od_w_in[i], od_w_out[i]
            p_lat, p_ctx = h_lat @ w_in, h_ctx @ w_in
            c_lat, c_ctx_o = mixer_c(p_lat[..., :3 * C_W], p_ctx[..., :3 * C_W], c_q_norm[i], c_k_norm[i], c_rpb[i], need_ctx)
            d_lat = conformer_conv(p_lat[..., 3 * C_W:], d_dw[i], d_dw_b[i], d_ln_g[i], d_ln_b[i])
            m_lat = jnp.concatenate([c_lat, d_lat], axis=-1)
            if need_ctx:
                d_ctx = conformer_conv(p_ctx[..., 3 * C_W:], d_dw[i], d_dw_b[i], d_ln_g[i], d_ln_b[i])
                m_ctx = jnp.concatenate([c_ctx_o, d_ctx], axis=-1)
        x_lat = x_lat + g1 * (m_lat @ w_out)
        h2_lat = rms_norm(x_lat, norm2_g[l]) * (1.0 + sc2) + sh2
        if need_ctx:
            x_ctx = x_ctx + cg1 * (m_ctx @ w_out)
            h2_ctx = rms_norm(x_ctx, norm2_g[l]) * (1.0 + cs2) + ch2
            toks = jnp.concatenate([h2_lat.reshape(b * s, d), h2_ctx.reshape(b * n_ctx, d)], axis=0)
            f = moe_ffn(toks, router_w, router_b, moe_w_gate[l], moe_w_up[l], moe_w_down[l])
            f_lat = f[:b * s].reshape(b, s, d)
            x_ctx = x_ctx + cg2 * f[b * s:].reshape(b, n_ctx, d)
        else:
            f_lat = moe_ffn(h2_lat.reshape(b * s, d), router_w, router_b, moe_w_gate[l], moe_w_up[l], moe_w_down[l]).reshape(b, s, d)
        x_lat = x_lat + g2 * f_lat
    return x_lat
```

```python
import functools

import numpy as np
import jax
import jax.numpy as jnp
from jax import lax
from jax.experimental import pallas as pl
from jax.experimental.pallas import tpu as pltpu

F32 = jnp.float32
BF16 = jnp.bfloat16

D_MODEL = 1024
DEPTH = 4
GRID_W = 64
HEAD_DIM = 64
EPS = 1e-6
A_HEADS = 8
A_KV_HEADS = 2
A_Q = A_HEADS * HEAD_DIM
A_KV = A_KV_HEADS * HEAD_DIM
A_IN = A_Q + 2 * A_KV
ROPE_THETA = 10000.0
ROPE_FREQS = HEAD_DIM // 4
B_GROUPS = 8
B_WIDTH = 512
CHUNK = 128
C_HEADS = 8
C_W = C_HEADS * HEAD_DIM
NA_ROWS = 8
NA_COLS = 16
D_CH = 512
CONV_W = 31
EVEN_IN = A_IN + 2 * B_WIDTH
ODD_IN = 3 * C_W + 2 * D_CH
N_EXPERTS = 16
N_GROUPS = 4
EXPERTS_PER_GROUP = 4
D_EXPERT = 512
MOE_BLOCK = 256

LANES = 128
HALF = 64
NEG_BIG = -1e30
CONV_HALO = 16


def _cparams(n_axes, vmem_mb):
    return pltpu.CompilerParams(dimension_semantics=("arbitrary",) * n_axes,
                                vmem_limit_bytes=vmem_mb << 20)


def _full(shape):
    nd = len(shape)
    return pl.BlockSpec(shape, lambda *_: (0,) * nd)


def _dot(a, b):
    return jnp.dot(a, b, preferred_element_type=F32)


def _dot_nt(a, b):
    return lax.dot_general(a, b, (((1,), (1,)), ((), ())), preferred_element_type=F32)


def _split_bf16(x):
    hi = x.astype(BF16)
    lo = (x - hi.astype(F32)).astype(BF16)
    return hi, lo


def _seg_mean_sq(x, bd):
    hi, lo = _split_bf16(x * x)
    return (_dot(hi, bd) + _dot(lo, bd)) * (1.0 / HALF)


def _rms_mod(x, g, sc, sh):
    ms = jnp.mean(x * x, axis=-1, keepdims=True)
    return x * lax.rsqrt(ms + EPS) * g * (1.0 + sc) + sh


def _rope(x, c, sg, lane):
    sw = jnp.where((lane % 32) < 16, pltpu.roll(x, LANES - 16, 1), pltpu.roll(x, 16, 1))
    return x * c + sw * sg


def _pair_lhs(qs, lane_row):
    ma = jnp.where(lane_row < HALF, 1.0, 0.0).astype(BF16)
    mb = jnp.where(lane_row < HALF, 0.0, 1.0).astype(BF16)
    return jnp.concatenate([qs * ma, qs * mb], axis=0)


def _mod_kernel(cc_ref, w_ref, b_ref, o_ref):
    cc = cc_ref[...]
    s = cc * jax.nn.sigmoid(cc)
    sh, sl = _split_bf16(s)
    wh, wl = _split_bf16(w_ref[0])
    o_ref[0] = _dot(sh, wh) + _dot(sl, wh) + _dot(sh, wl) + b_ref[0]


def _modulation(cc, ada_w, ada_b):
    tn = 1024
    return pl.pallas_call(
        _mod_kernel,
        grid=(DEPTH, 6 * D_MODEL // tn),
        in_specs=[pl.BlockSpec((8, D_MODEL), lambda l, j: (0, 0)),
                  pl.BlockSpec((1, D_MODEL, tn), lambda l, j: (l, 0, j)),
                  pl.BlockSpec((1, 1, tn), lambda l, j: (l, 0, j))],
        out_specs=pl.BlockSpec((1, 8, tn), lambda l, j: (l, 0, j)),
        out_shape=jax.ShapeDtypeStruct((DEPTH, 8, 6 * D_MODEL), F32),
        compiler_params=_cparams(2, 32),
        name="modulation",
    )(cc, ada_w, ada_b.reshape(DEPTH, 1, 6 * D_MODEL))


def _proj_even_kernel(x_ref, g_ref, sc_ref, sh_ref, w_ref, bd512_ref, bd128_ref, qg_ref, kg_ref, vg_ref,
                      cos_ref, sin_ref, ws_ref, bsb_ref, q_ref, kd_ref, v_ref, b_ref, *, ts):
    h = _rms_mod(x_ref[...], g_ref[...], sc_ref[...], sh_ref[...])
    p = _dot(h.astype(BF16), w_ref[...])
    lane = lax.broadcasted_iota(jnp.int32, (ts, LANES), 1)
    c = cos_ref[...]
    sg = sin_ref[...]

    q = p[:, :A_Q]
    qn = q * lax.rsqrt(_seg_mean_sq(q, bd512_ref[...]) + EPS) * qg_ref[...]
    for sl in range(A_Q // LANES):
        xs = qn[:, sl * LANES:(sl + 1) * LANES]
        q_ref[:, sl * LANES:(sl + 1) * LANES] = (_rope(xs, c, sg, lane) * (HEAD_DIM ** -0.5)).astype(BF16)

    k = p[:, A_Q:A_Q + A_KV]
    kn = k * lax.rsqrt(_seg_mean_sq(k, bd128_ref[...]) + EPS) * kg_ref[...]
    kt = _rope(kn, c, sg, lane).T
    for j in range(A_KV_HEADS):
        kj = kt[j * HALF:(j + 1) * HALF]
        kd_ref[j] = jnp.concatenate([kj, kj], axis=0).astype(BF16)
    v_ref[...] = p[:, A_Q + A_KV:A_IN].astype(BF16)

    zu = jax.nn.gelu(p[:, A_IN:A_IN + B_WIDTH])
    zv = jax.nn.gelu(p[:, A_IN + B_WIDTH:])
    vn = (zv * lax.rsqrt(_seg_mean_sq(zv, bd512_ref[...]) + EPS) * vg_ref[...]).astype(BF16)
    lane_c = lax.broadcasted_iota(jnp.int32, (CHUNK, LANES), 1)
    for ch in range(ts // CHUNK):
        rows = slice(ch * CHUNK, (ch + 1) * CHUNK)
        for sl in range(B_WIDTH // LANES):
            cols = slice(sl * LANES, (sl + 1) * LANES)
            vs = vn[rows, cols]
            mixed = jnp.where(lane_c < HALF, _dot(ws_ref[2 * sl], vs), _dot(ws_ref[2 * sl + 1], vs))
            b_ref[rows, cols] = (zu[rows, cols] * (mixed + bsb_ref[:, cols])).astype(BF16)


def _proj_even(x, mods, w_in, consts, cos, sin, ts):
    n = x.shape[0]
    g, sc, sh = mods
    bd512, bd128, qg, kg, vg, ws, bsb = consts
    row = lambda i: (i, 0)
    vec = lambda w: pl.BlockSpec((1, w), lambda i: (0, 0))
    return pl.pallas_call(
        functools.partial(_proj_even_kernel, ts=ts),
        grid=(n // ts,),
        in_specs=[pl.BlockSpec((ts, D_MODEL), row), vec(D_MODEL), vec(D_MODEL), vec(D_MODEL),
                  _full((D_MODEL, EVEN_IN)), _full((A_Q, A_Q)), _full((LANES, LANES)),
                  vec(A_Q), vec(A_KV), vec(B_WIDTH),
                  pl.BlockSpec((ts, LANES), row), pl.BlockSpec((ts, LANES), row),
                  _full((B_GROUPS, CHUNK, CHUNK)), _full((CHUNK, B_WIDTH))],
        out_specs=[pl.BlockSpec((ts, A_Q), row),
                   pl.BlockSpec((A_KV_HEADS, LANES, ts), lambda i: (0, 0, i)),
                   pl.BlockSpec((ts, A_KV), row),
                   pl.BlockSpec((ts, B_WIDTH), row)],
        out_shape=[jax.ShapeDtypeStruct((n, A_Q), BF16),
                   jax.ShapeDtypeStruct((A_KV_HEADS, LANES, n), BF16),
                   jax.ShapeDtypeStruct((n, A_KV), BF16),
                   jax.ShapeDtypeStruct((n, B_WIDTH), BF16)],
        compiler_params=_cparams(1, 56),
        name="proj_even",
    )(x, g, sc, sh, w_in, bd512, bd128, qg, kg, vg, cos, sin, ws, bsb)


def _attn_even_kernel(*refs, n_src, tq, tks):
    q_ref = refs[0]
    srcs = [(refs[1 + 2 * i], refs[2 + 2 * i]) for i in range(n_src)]
    o_ref = refs[1 + 2 * n_src]
    lane_row = lax.broadcasted_iota(jnp.int32, (1, LANES), 1)
    lane = lax.broadcasted_iota(jnp.int32, (tq, LANES), 1)
    for j in range(A_KV_HEADS):
        lhs = jnp.concatenate([_pair_lhs(q_ref[:, sl * LANES:(sl + 1) * LANES], lane_row)
                               for sl in (2 * j, 2 * j + 1)], axis=0)
        carry = (jnp.full((4 * tq, 1), NEG_BIG, F32), jnp.zeros((4 * tq, 1), F32),
                 jnp.zeros((4 * tq, LANES), F32))
        for (kd_ref, v_ref), tk in zip(srcs, tks):

            def body(ci, carry, kd_ref=kd_ref, v_ref=v_ref, tk=tk):
                m, l, acc = carry
                off = pl.multiple_of(ci * tk, tk)
                s = _dot(lhs, kd_ref[j, :, pl.ds(off, tk)])
                m_new = jnp.maximum(m, jnp.max(s, axis=-1, keepdims=True))
                alpha = jnp.exp(m - m_new)
                p = jnp.exp(s - m_new)
                l = alpha * l + jnp.sum(p, axis=-1, keepdims=True)
                acc = alpha * acc + _dot(p.astype(BF16), v_ref[pl.ds(off, tk), :])
                return m_new, l, acc

            n_chunks = kd_ref.shape[2] // tk
            carry = body(0, carry) if n_chunks == 1 else lax.fori_loop(0, n_chunks, body, carry)
        _, l, acc = carry
        o = acc / l
        for t, sl in enumerate((2 * j, 2 * j + 1)):
            oa = o[(2 * t) * tq:(2 * t + 1) * tq]
            ob = o[(2 * t + 1) * tq:(2 * t + 2) * tq]
            if j == 0:
                slab = jnp.where(lane < HALF, oa, pltpu.roll(ob, HALF, 1))
            else:
                slab = jnp.where(lane < HALF, pltpu.roll(oa, HALF, 1), ob)
            o_ref[:, sl * LANES:(sl + 1) * LANES] = slab.astype(BF16)


def _attn_even(q, srcs, tq, tks):
    n = q.shape[0]
    in_specs = [pl.BlockSpec((tq, A_Q), lambda i: (i, 0))]
    args = [q]
    for kd, v in srcs:
        in_specs += [_full(kd.shape), _full(v.shape)]
        args += [kd, v]
    return pl.pallas_call(
        functools.partial(_attn_even_kernel, n_src=len(srcs), tq=tq, tks=tuple(tks)),
        grid=(n // tq,),
        in_specs=in_specs,
        out_specs=pl.BlockSpec((tq, A_Q), lambda i: (i, 0)),
        out_shape=jax.ShapeDtypeStruct((n, A_Q), BF16),
        compiler_params=_cparams(1, 56),
        name="attn_even",
    )(*args)


def _proj_odd_kernel(x_ref, g_ref, sc_ref, sh_ref, w_ref, bd512_ref, qg_ref, kg_ref,
                     q_ref, k_ref, v_ref, y_ref):
    h = _rms_mod(x_ref[...], g_ref[...], sc_ref[...], sh_ref[...])
    p = _dot(h.astype(BF16), w_ref[...])
    bd = bd512_ref[...]
    q = p[:, :C_W]
    q_ref[...] = (q * lax.rsqrt(_seg_mean_sq(q, bd) + EPS) * qg_ref[...] * (HEAD_DIM ** -0.5)).astype(BF16)
    k = p[:, C_W:2 * C_W]
    k_ref[...] = (k * lax.rsqrt(_seg_mean_sq(k, bd) + EPS) * kg_ref[...]).astype(BF16)
    v_ref[...] = p[:, 2 * C_W:3 * C_W].astype(BF16)
    y_ref[...] = p[:, 3 * C_W:3 * C_W + D_CH] * jax.nn.sigmoid(p[:, 3 * C_W + D_CH:])


def _proj_odd(x, mods, w_in, consts, ts):
    n = x.shape[0]
    g, sc, sh = mods
    bd512, qg, kg = consts
    row = lambda i: (i, 0)
    vec = lambda w: pl.BlockSpec((1, w), lambda i: (0, 0))
    blk = pl.BlockSpec((ts, C_W), row)
    return pl.pallas_call(
        _proj_odd_kernel,
        grid=(n // ts,),
        in_specs=[pl.BlockSpec((ts, D_MODEL), row), vec(D_MODEL), vec(D_MODEL), vec(D_MODEL),
                  _full((D_MODEL, ODD_IN)), _full((C_W, C_W)), vec(C_W), vec(C_W)],
        out_specs=[blk, blk, blk, blk],
        out_shape=[jax.ShapeDtypeStruct((n, C_W), BF16), jax.ShapeDtypeStruct((n, C_W), BF16),
                   jax.ShapeDtypeStruct((n, C_W), BF16), jax.ShapeDtypeStruct((n, D_CH), F32)],
        compiler_params=_cparams(1, 56),
        name="proj_odd",
    )(x, g, sc, sh, w_in, bd512, qg, kg)


NA_BLOCK_ROWS = 8
NA_TOK = NA_BLOCK_ROWS * GRID_W
NA_WIN = NA_ROWS * GRID_W


def _na_kernel(q_ref, kp_ref, kc_ref, kn_ref, vp_ref, vc_ref, vn_ref, kx_ref, vx_ref, bias_ref, o_ref,
               kbuf, vbuf, *, rows):
    i = pl.program_id(0)
    for t, (kr, vr) in enumerate(((kp_ref, vp_ref), (kc_ref, vc_ref), (kn_ref, vn_ref))):
        kbuf[t * NA_TOK:(t + 1) * NA_TOK] = kr[...]
        vbuf[t * NA_TOK:(t + 1) * NA_TOK] = vr[...]
    lane_row = lax.broadcasted_iota(jnp.int32, (1, LANES), 1)
    lane = lax.broadcasted_iota(jnp.int32, (GRID_W, LANES), 1)
    for sl in range(C_W // LANES):
        cols = slice(sl * LANES, (sl + 1) * LANES)
        lhs = _pair_lhs(q_ref[:, cols], lane_row)
        s_ctx = _dot_nt(lhs, kx_ref[:, cols])
        sa, sb, starts = [], [], []
        for j in range(NA_BLOCK_ROWS):
            r = i * NA_BLOCK_ROWS + j
            r0 = jnp.clip(r - NA_ROWS // 2, 0, rows - NA_ROWS)
            start = pl.multiple_of((r0 - (i - 1) * NA_BLOCK_ROWS) * GRID_W, GRID_W)
            starts.append(start)
            qa = lhs[j * GRID_W:(j + 1) * GRID_W]
            qb = lhs[NA_TOK + j * GRID_W:NA_TOK + (j + 1) * GRID_W]
            kw = kbuf[pl.ds(start, NA_WIN), cols]
            s = _dot_nt(jnp.concatenate([qa, qb], axis=0), kw) + bias_ref[r0 - r + NA_ROWS - 1, sl]
            sa.append(s[:GRID_W])
            sb.append(s[GRID_W:])
        s_loc = jnp.concatenate(sa + sb, axis=0)
        m = jnp.maximum(jnp.max(s_loc, axis=-1, keepdims=True), jnp.max(s_ctx, axis=-1, keepdims=True))
        p_loc = jnp.exp(s_loc - m)
        p_ctx = jnp.exp(s_ctx - m)
        l = jnp.sum(p_loc, axis=-1, keepdims=True) + jnp.sum(p_ctx, axis=-1, keepdims=True)
        o_ctx = _dot(p_ctx.astype(BF16), vx_ref[:, cols])
        p_loc = p_loc.astype(BF16)
        for j in range(NA_BLOCK_ROWS):
            ra = slice(j * GRID_W, (j + 1) * GRID_W)
            rb = slice(NA_TOK + j * GRID_W, NA_TOK + (j + 1) * GRID_W)
            vw = vbuf[pl.ds(starts[j], NA_WIN), cols]
            o2 = _dot(jnp.concatenate([p_loc[ra], p_loc[rb]], axis=0), vw)
            oa = (o2[:GRID_W] + o_ctx[ra]) / l[ra]
            ob = (o2[GRID_W:] + o_ctx[rb]) / l[rb]
            o_ref[ra, cols] = jnp.where(lane < HALF, oa, ob).astype(BF16)


def _na_attention(q, k, v, kx, vx, bias):
    n = q.shape[0]
    nb = n // NA_TOK
    blk = lambda f: pl.BlockSpec((NA_TOK, C_W), f)
    prev = lambda i: (jnp.maximum(i - 1, 0), 0)
    cur = lambda i: (i, 0)
    nxt = lambda i: (jnp.minimum(i + 1, nb - 1), 0)
    return pl.pallas_call(
        functools.partial(_na_kernel, rows=n // GRID_W),
        grid=(nb,),
        in_specs=[blk(cur), blk(prev), blk(cur), blk(nxt), blk(prev), blk(cur), blk(nxt),
                  _full(kx.shape), _full(vx.shape), _full(bias.shape)],
        out_specs=blk(cur),
        out_shape=jax.ShapeDtypeStruct((n, C_W), BF16),
        scratch_shapes=[pltpu.VMEM((3 * NA_TOK, C_W), BF16), pltpu.VMEM((3 * NA_TOK, C_W), BF16)],
        compiler_params=_cparams(1, 56),
        name="na_attention",
    )(q, k, k, k, v, v, v, kx, vx, bias)


def _na_bias(rpb):
    cc = np.arange(GRID_W)
    c0 = np.clip(cc - NA_COLS // 2, 0, GRID_W - NA_COLS)
    kc = np.arange(GRID_W)
    inwin = (kc[None, :] >= c0[:, None]) & (kc[None, :] < c0[:, None] + NA_COLS)
    dc = np.clip(kc[None, :] - cc[:, None] + NA_COLS - 1, 0, 2 * NA_COLS - 2)
    dr = np.arange(NA_ROWS)[:, None] + np.arange(NA_ROWS)[None, :]
    t = rpb.astype(F32)[:, dr][:, :, :, dc]
    t = jnp.where(jnp.asarray(inwin)[None, None, None], t, NEG_BIG)
    t = jnp.transpose(t, (1, 0, 3, 2, 4))
    return t.reshape(NA_ROWS, C_HEADS // 2, 2 * GRID_W, NA_WIN)


def _mha_kernel(q_ref, k_ref, v_ref, o_ref):
    n = q_ref.shape[0]
    lane_row = lax.broadcasted_iota(jnp.int32, (1, LANES), 1)
    lane = lax.broadcasted_iota(jnp.int32, (n, LANES), 1)
    for sl in range(C_W // LANES):
        cols = slice(sl * LANES, (sl + 1) * LANES)
        s = _dot_nt(_pair_lhs(q_ref[:, cols], lane_row), k_ref[:, cols])
        p = jnp.exp(s - jnp.max(s, axis=-1, keepdims=True))
        l = jnp.sum(p, axis=-1, keepdims=True)
        o = _dot(p.astype(BF16), v_ref[:, cols]) / l
        o_ref[:, cols] = jnp.where(lane < HALF, o[:n], o[n:]).astype(BF16)


def _mha_small(q, k, v):
    return pl.pallas_call(
        _mha_kernel,
        grid=(1,),
        in_specs=[_full(q.shape), _full(k.shape), _full(v.shape)],
        out_specs=_full(q.shape),
        out_shape=jax.ShapeDtypeStruct(q.shape, BF16),
        compiler_params=_cparams(1, 32),
        name="mha_small",
    )(q, k, v)


def _conv_kernel(y_ref, yp_ref, yn_ref, dw_ref, dwb_ref, lng_ref, lnb_ref, o_ref, ext, *, ts):
    i = pl.program_id(0)
    nb = pl.num_programs(0)
    ext[0:CONV_HALO] = jnp.where(i > 0, yp_ref[...], 0.0)
    ext[CONV_HALO:CONV_HALO + ts] = y_ref[...]
    ext[CONV_HALO + ts:] = jnp.where(i < nb - 1, yn_ref[...], 0.0)
    acc = jnp.zeros((ts, D_CH), F32)
    base = CONV_HALO - CONV_W // 2
    for j in range(CONV_W):
        acc = acc + ext[base + j:base + j + ts, :] * dw_ref[j:j + 1, :]
    yb = acc + dwb_ref[...]
    mu = jnp.mean(yb, axis=-1, keepdims=True)
    var = jnp.mean(jnp.square(yb - mu), axis=-1, keepdims=True)
    z = (yb - mu) * lax.rsqrt(var + EPS) * lng_ref[...] + lnb_ref[...]
    o_ref[...] = (z * jax.nn.sigmoid(z)).astype(BF16)


def _conv_module(y, dw, dwb, lng, lnb, ts):
    n = y.shape[0]
    hb = ts // CONV_HALO
    nh = n // CONV_HALO
    vec = pl.BlockSpec((1, D_CH), lambda i: (0, 0))
    return pl.pallas_call(
        functools.partial(_conv_kernel, ts=ts),
        grid=(n // ts,),
        in_specs=[pl.BlockSpec((ts, D_CH), lambda i: (i, 0)),
                  pl.BlockSpec((CONV_HALO, D_CH), lambda i: (jnp.maximum(i * hb - 1, 0), 0)),
                  pl.BlockSpec((CONV_HALO, D_CH), lambda i: (jnp.minimum((i + 1) * hb, nh - 1), 0)),
                  _full((CONV_W + 1, D_CH)), vec, vec, vec],
        out_specs=pl.BlockSpec((ts, D_CH), lambda i: (i, 0)),
        out_shape=jax.ShapeDtypeStruct((n, D_CH), BF16),
        scratch_shapes=[pltpu.VMEM((ts + 2 * CONV_HALO, D_CH), F32)],
        compiler_params=_cparams(1, 32),
        name="conv_module",
    )(y, y, y, dw, dwb, lng, lnb)


def _route(logits, rb, lane):
    aff = jax.nn.sigmoid(logits)
    sel = aff + rb
    big = 4 * LANES

    def top2(valid):
        v = jnp.where(valid, sel, -jnp.inf)
        m1 = jnp.max(v, axis=-1, keepdims=True)
        i1 = jnp.min(jnp.where(v == m1, lane, big), axis=-1, keepdims=True)
        v2 = jnp.where(lane == i1, -jnp.inf, v)
        m2 = jnp.max(v2, axis=-1, keepdims=True)
        i2 = jnp.min(jnp.where(v2 == m2, lane, big), axis=-1, keepdims=True)
        return m1, m2, i1, i2

    best = None
    for g in range(N_GROUPS):
        m1, m2, _, _ = top2((lane >= g * EXPERTS_PER_GROUP) & (lane < (g + 1) * EXPERTS_PER_GROUP))
        score = m1 + m2
        if best is None:
            best, bg = score, jnp.zeros_like(score, dtype=jnp.int32)
        else:
            upd = score > best
            best = jnp.where(upd, score, best)
            bg = jnp.where(upd, g, bg)
    lo = bg * EXPERTS_PER_GROUP
    _, _, i1, i2 = top2((lane >= lo) & (lane < lo + EXPERTS_PER_GROUP))
    a1 = jnp.sum(jnp.where(lane == i1, aff, 0.0), axis=-1, keepdims=True)
    a2 = jnp.sum(jnp.where(lane == i2, aff, 0.0), axis=-1, keepdims=True)
    tot = a1 + a2
    out = jnp.where(lane == 0, i1.astype(F32), 0.0)
    out = jnp.where(lane == 1, i2.astype(F32), out)
    out = jnp.where(lane == 2, a1 / tot, out)
    return jnp.where(lane == 3, a2 / tot, out)


def _post_mix_kernel(x_ref, a_ref, b_ref, w_ref, g1_ref, ng_ref, sc_ref, sh_ref, rwh_ref, rwl_ref, rb_ref,
                     xo_ref, h2_ref, route_ref, *, ts):
    half = w_ref.shape[0] // 2
    mo = _dot(a_ref[...], w_ref[:half]) + _dot(b_ref[...], w_ref[half:])
    xn = x_ref[...] + g1_ref[...] * mo
    xo_ref[...] = xn
    h2 = _rms_mod(xn, ng_ref[...], sc_ref[...], sh_ref[...])
    h2_ref[...] = h2
    hh, hl = _split_bf16(h2)
    logits = _dot(hh, rwh_ref[...]) + _dot(hl, rwh_ref[...]) + _dot(hh, rwl_ref[...])
    lane = lax.broadcasted_iota(jnp.int32, (ts, LANES), 1)
    route_ref[...] = _route(logits, rb_ref[...], lane)


def _post_mix(x, a, b, w_out, mods, router, ts):
    n = x.shape[0]
    g1, ng, sc, sh = mods
    rwh, rwl, rb = router
    row = lambda i: (i, 0)
    vec = lambda w: pl.BlockSpec((1, w), lambda i: (0, 0))
    return pl.pallas_call(
        functools.partial(_post_mix_kernel, ts=ts),
        grid=(n // ts,),
        in_specs=[pl.BlockSpec((ts, D_MODEL), row), pl.BlockSpec((ts, a.shape[1]), row),
                  pl.BlockSpec((ts, b.shape[1]), row), _full(w_out.shape),
                  vec(D_MODEL), vec(D_MODEL), vec(D_MODEL), vec(D_MODEL),
                  _full(rwh.shape), _full(rwl.shape), vec(LANES)],
        out_specs=[pl.BlockSpec((ts, D_MODEL), row), pl.BlockSpec((ts, D_MODEL), row),
                   pl.BlockSpec((ts, LANES), row)],
        out_shape=[jax.ShapeDtypeStruct((n, D_MODEL), F32), jax.ShapeDtypeStruct((n, D_MODEL), F32),
                   jax.ShapeDtypeStruct((n, LANES), F32)],
        compiler_params=_cparams(1, 56),
        name="post_mix",
    )(x, a, b, w_out, g1, ng, sc, sh, rwh, rwl, rb)


def _dispatch_kernel(dest_ref, h_ref, buf_in_ref, buf_ref, sem, *, ts):
    del buf_in_ref

    def row_copy(r, k):
        return pltpu.make_async_copy(h_ref.at[pl.ds(r, 1)], buf_ref.at[pl.ds(dest_ref[0, 0, 2 * r + k], 1)], sem)

    def issue(r, carry):
        row_copy(r, 0).start()
        row_copy(r, 1).start()
        return carry

    def drain(r, carry):
        row_copy(r, 0).wait()
        row_copy(r, 1).wait()
        return carry

    lax.fori_loop(0, ts, issue, 0)
    lax.fori_loop(0, ts, drain, 0)


def _dispatch(h2, dest3, n_rows, ts):
    n = h2.shape[0]
    buf0 = jnp.zeros((n_rows, D_MODEL), F32)
    return pl.pallas_call(
        functools.partial(_dispatch_kernel, ts=ts),
        grid=(n // ts,),
        in_specs=[pl.BlockSpec((1, 1, 2 * ts), lambda i: (i, 0, 0), memory_space=pltpu.SMEM),
                  pl.BlockSpec((ts, D_MODEL), lambda i: (i, 0)),
                  pl.BlockSpec(memory_space=pl.ANY)],
        out_specs=pl.BlockSpec(memory_space=pl.ANY),
        out_shape=jax.ShapeDtypeStruct((n_rows, D_MODEL), F32),
        scratch_shapes=[pltpu.SemaphoreType.DMA(())],
        input_output_aliases={2: 0},
        compiler_params=_cparams(1, 32),
        name="moe_dispatch",
    )(dest3, h2, buf0)


def _expert_kernel(blk_e_ref, n_used_ref, x_ref, wg_ref, wu_ref, wd_ref, y_ref):
    del blk_e_ref
    i = pl.program_id(0)

    @pl.when(i < n_used_ref[0])
    def _():
        xb = x_ref[...].astype(BF16)
        gate = _dot(xb, wg_ref[0])
        up = _dot(xb, wu_ref[0])
        hmid = gate * jax.nn.sigmoid(gate) * up
        y_ref[...] = _dot(hmid.astype(BF16), wd_ref[0])

    @pl.when(i >= n_used_ref[0])
    def _():
        y_ref[...] = jnp.zeros_like(y_ref)


def _experts(buf, blk_e, n_used, wg, wu, wd):
    n_rows = buf.shape[0]
    nb = n_rows // MOE_BLOCK
    grid_spec = pltpu.PrefetchScalarGridSpec(
        num_scalar_prefetch=2,
        grid=(nb,),
        in_specs=[pl.BlockSpec((MOE_BLOCK, D_MODEL), lambda i, be, nu: (i, 0)),
                  pl.BlockSpec((1, D_MODEL, D_EXPERT), lambda i, be, nu: (be[i], 0, 0)),
                  pl.BlockSpec((1, D_MODEL, D_EXPERT), lambda i, be, nu: (be[i], 0, 0)),
                  pl.BlockSpec((1, D_EXPERT, D_MODEL), lambda i, be, nu: (be[i], 0, 0))],
        out_specs=pl.BlockSpec((MOE_BLOCK, D_MODEL), lambda i, be, nu: (i, 0)),
    )
    return pl.pallas_call(
        _expert_kernel,
        grid_spec=grid_spec,
        out_shape=jax.ShapeDtypeStruct((n_rows, D_MODEL), F32),
        compiler_params=_cparams(1, 48),
        name="moe_experts",
    )(blk_e, n_used, buf, wg, wu, wd)


def _combine_kernel(dest_ref, x_ref, route_ref, g2_ref, y_ref, o_ref, g0, g1, sem, *, ts):
    def row_copy(r, k):
        dst = g0 if k == 0 else g1
        return pltpu.make_async_copy(y_ref.at[pl.ds(dest_ref[0, 0, 2 * r + k], 1)], dst.at[pl.ds(r, 1)], sem)

    def issue(r, carry):
        row_copy(r, 0).start()
        row_copy(r, 1).start()
        return carry

    def drain(r, carry):
        row_copy(r, 0).wait()
        row_copy(r, 1).wait()
        return carry

    lax.fori_loop(0, ts, issue, 0)
    lax.fori_loop(0, ts, drain, 0)
    route = route_ref[...]
    f = g0[...] * route[:, 2:3] + g1[...] * route[:, 3:4]
    o_ref[...] = x_ref[...] + g2_ref[...] * f


def _combine(x, route, g2, y, dest3, ts):
    n = x.shape[0]
    row = lambda i: (i, 0)
    return pl.pallas_call(
        functools.partial(_combine_kernel, ts=ts),
        grid=(n // ts,),
        in_specs=[pl.BlockSpec((1, 1, 2 * ts), lambda i: (i, 0, 0), memory_space=pltpu.SMEM),
                  pl.BlockSpec((ts, D_MODEL), row), pl.BlockSpec((ts, LANES), row),
                  pl.BlockSpec((1, D_MODEL), lambda i: (0, 0)),
                  pl.BlockSpec(memory_space=pl.ANY)],
        out_specs=pl.BlockSpec((ts, D_MODEL), row),
        out_shape=jax.ShapeDtypeStruct((n, D_MODEL), F32),
        scratch_shapes=[pltpu.VMEM((ts, D_MODEL), F32), pltpu.VMEM((ts, D_MODEL), F32),
                        pltpu.SemaphoreType.DMA(())],
        compiler_params=_cparams(1, 48),
        name="moe_combine",
    )(dest3, x, route, g2, y)


def _moe(x, h2, route, g2, wg, wu, wd, ts):
    n = x.shape[0]
    e_flat = route[:, :2].astype(jnp.int32).reshape(2 * n)
    onehot = (e_flat[:, None] == jnp.arange(N_EXPERTS, dtype=jnp.int32)[None, :]).astype(jnp.int32)
    csum = jnp.cumsum(onehot, axis=0)
    rank = jnp.sum((csum - onehot) * onehot, axis=1)
    counts = csum[-1]
    pcounts = (counts + MOE_BLOCK - 1) // MOE_BLOCK * MOE_BLOCK
    pend = jnp.cumsum(pcounts)
    pstart = pend - pcounts
    dest = jnp.sum(onehot * pstart[None, :], axis=1) + rank
    n_blocks = -(-2 * n // MOE_BLOCK) + N_EXPERTS
    blk_e = jnp.minimum(jnp.searchsorted(pend, jnp.arange(n_blocks, dtype=jnp.int32) * MOE_BLOCK, side="right"),
                        N_EXPERTS - 1).astype(jnp.int32)
    n_used = (pend[-1:] // MOE_BLOCK).astype(jnp.int32)
    dest3 = dest.astype(jnp.int32).reshape(n // ts, 1, 2 * ts)
    buf = _dispatch(h2, dest3, n_blocks * MOE_BLOCK, ts)
    y = _experts(buf, blk_e, n_used, wg, wu, wd)
    return _combine(x, route, g2, y, dest3, ts)


def _block_diag_ones(width):
    idx = np.arange(width) // HALF
    return jnp.asarray((idx[:, None] == idx[None, :]).astype(np.float32), dtype=BF16)


def _rope_tables(n_tok):
    t = jnp.arange(n_tok, dtype=jnp.int32)
    pos = jnp.stack([t // GRID_W, t % GRID_W], axis=-1).astype(F32)
    inv = ROPE_THETA ** (-jnp.arange(ROPE_FREQS, dtype=F32) / ROPE_FREQS)
    ang = pos[:, :, None] * inv
    cos, sin = jnp.cos(ang), jnp.sin(ang)
    c = jnp.concatenate([cos, cos], axis=-1).reshape(n_tok, HEAD_DIM)
    s = jnp.concatenate([-sin, sin], axis=-1).reshape(n_tok, HEAD_DIM)
    return jnp.tile(c, (1, 2)), jnp.tile(s, (1, 2))


def _tile_vec(v, reps):
    return jnp.tile(v.astype(F32), reps).reshape(1, -1)


def kernel(x, c, ctx, c_ctx, router_w, router_b, ada_w, ada_b, norm1_g, norm2_g, ev_w_in, ev_w_out, a_q_norm,
           a_k_norm, b_v_norm, b_ws, b_bs, od_w_in, od_w_out, c_q_norm, c_k_norm, c_rpb, d_dw, d_dw_b, d_ln_g,
           d_ln_b, moe_w_gate, moe_w_up, moe_w_down):
    assert x.shape[0] == 1 and ctx.shape[0] == 1
    s_len, n_ctx = x.shape[1], ctx.shape[1]
    ts_lat = min(512, s_len)
    ts_ctx = n_ctx
    x_lat = x[0]
    x_ctx = ctx[0]

    cc = jnp.zeros((8, D_MODEL), F32).at[0].set(c[0]).at[1].set(c_ctx)
    mods = _modulation(cc, ada_w, ada_b)

    def mod(l, row, k):
        return mods[l, row, k * D_MODEL:(k + 1) * D_MODEL].reshape(1, D_MODEL)

    bd512 = _block_diag_ones(A_Q)
    bd128 = _block_diag_ones(LANES)
    cos_lat, sin_lat = _rope_tables(s_len)
    cos_ctx, sin_ctx = jnp.ones((n_ctx, LANES), F32), jnp.zeros((n_ctx, LANES), F32)
    rw = jnp.zeros((D_MODEL, LANES), F32).at[:, :N_EXPERTS].set(router_w)
    rwh = rw.astype(BF16)
    rwl = (rw - rwh.astype(F32)).astype(BF16)
    rb = jnp.zeros((1, LANES), F32).at[0, :N_EXPERTS].set(router_b)
    router = (rwh, rwl, rb)

    for l in range(DEPTH):
        need_ctx = l < DEPTH - 1
        i = l // 2
        g_n1 = norm1_g[l].reshape(1, D_MODEL)
        g_n2 = norm2_g[l].reshape(1, D_MODEL)
        m_lat = [mod(l, 0, k) for k in range(6)]
        m_ctx = [mod(l, 1, k) for k in range(6)]
        if l % 2 == 0:
            w_in = ev_w_in[i].astype(BF16)
            w_out = ev_w_out[i].astype(BF16)
            consts = (bd512, bd128, _tile_vec(a_q_norm[i], A_HEADS), _tile_vec(a_k_norm[i], A_KV_HEADS),
                      b_v_norm[i].reshape(1, B_WIDTH).astype(F32), b_ws[i].astype(BF16),
                      jnp.repeat(b_bs[i].T, HALF, axis=1))
            q_l, kd_l, v_l, b_l = _proj_even(x_lat, (g_n1, m_lat[1], m_lat[0]), w_in, consts, cos_lat, sin_lat, ts_lat)
            q_c, kd_c, v_c, b_c = _proj_even(x_ctx, (g_n1, m_ctx[1], m_ctx[0]), w_in, consts, cos_ctx, sin_ctx, ts_ctx)
            a_l = _attn_even(q_l, [(kd_c, v_c), (kd_l, v_l)], min(256, s_len), [n_ctx, min(512, s_len)])
            mix_l = (a_l, b_l)
            if need_ctx:
                mix_c = (_attn_even(q_c, [(kd_c, v_c)], n_ctx, [n_ctx]), b_c)
        else:
            w_in = od_w_in[i].astype(BF16)
            w_out = od_w_out[i].astype(BF16)
            consts = (bd512, _tile_vec(c_q_norm[i], C_HEADS), _tile_vec(c_k_norm[i], C_HEADS))
            q_l, k_l, v_l, y_l = _proj_odd(x_lat, (g_n1, m_lat[1], m_lat[0]), w_in, consts, ts_lat)
            q_c, k_c, v_c, y_c = _proj_odd(x_ctx, (g_n1, m_ctx[1], m_ctx[0]), w_in, consts, ts_ctx)
            dw = jnp.zeros((CONV_W + 1, D_CH), F32).at[:CONV_W].set(d_dw[i])
            conv_p = (dw, d_dw_b[i].reshape(1, D_CH), d_ln_g[i].reshape(1, D_CH), d_ln_b[i].reshape(1, D_CH))
            mix_l = (_na_attention(q_l, k_l, v_l, k_c, v_c, _na_bias(c_rpb[i])), _conv_module(y_l, *conv_p, ts_lat))
            if need_ctx:
                mix_c = (_mha_small(q_c, k_c, v_c), _conv_module(y_c, *conv_p, ts_ctx))
        wg = moe_w_gate[l].astype(BF16)
        wu = moe_w_up[l].astype(BF16)
        wd = moe_w_down[l].astype(BF16)
        xn, h2, route = _post_mix(x_lat, *mix_l, w_out, (m_lat[2], g_n2, m_lat[4], m_lat[3]), router, ts_lat)
        x_lat = _moe(xn, h2, route, m_lat[5], wg, wu, wd, ts_lat)
        if need_ctx:
            xn, h2, route = _post_mix(x_ctx, *mix_c, w_out, (m_ctx[2], g_n2, m_ctx[4], m_ctx[3]), router, ts_ctx)
            x_ctx = _moe(xn, h2, route, m_ctx[5], wg, wu, wd, ts_ctx)
    return x_lat[None]
```

```python
import functools

import numpy as np
import jax
import jax.numpy as jnp
from jax import lax
from jax.experimental import pallas as pl
from jax.experimental.pallas import tpu as pltpu

F32 = jnp.float32
BF16 = jnp.bfloat16

D_MODEL = 1024
DEPTH = 4
GRID_W = 64
HEAD_DIM = 64
EPS = 1e-6
A_HEADS = 8
A_KV_HEADS = 2
A_Q = A_HEADS * HEAD_DIM
A_KV = A_KV_HEADS * HEAD_DIM
A_IN = A_Q + 2 * A_KV
ROPE_THETA = 10000.0
ROPE_FREQS = HEAD_DIM // 4
B_GROUPS = 8
B_WIDTH = 512
CHUNK = 128
C_HEADS = 8
C_W = C_HEADS * HEAD_DIM
NA_ROWS = 8
NA_COLS = 16
D_CH = 512
CONV_W = 31
EVEN_IN = A_IN + 2 * B_WIDTH
ODD_IN = 3 * C_W + 2 * D_CH
N_EXPERTS = 16
N_GROUPS = 4
EXPERTS_PER_GROUP = 4
D_EXPERT = 512
MOE_BLOCK = 256

LANES = 128
HALF = 64
NEG_BIG = -1e30
CONV_HALO = 16
DMA_UNROLL = 8
ATTN_TQ = 256
ATTN_TK = 512
QK_SCALE_LOG2 =HEAD_DIM ** -0.5 * 1.4426950408889634


def _cparams(n_axes, vmem_mb):
    return pltpu.CompilerParams(dimension_semantics=("arbitrary",) * n_axes,
                                vmem_limit_bytes=vmem_mb << 20)


def _full(shape):
    nd = len(shape)
    return pl.BlockSpec(shape, lambda *_: (0,) * nd)


def _dot(a, b):
    return jnp.dot(a, b, preferred_element_type=F32)


def _dot_nt(a, b):
    return lax.dot_general(a, b, (((1,), (1,)), ((), ())), preferred_element_type=F32)


def _split_bf16(x):
    hi = x.astype(BF16)
    lo = (x - hi.astype(F32)).astype(BF16)
    return hi, lo


def _seg_mean_sq(x, bd):
    hi, lo = _split_bf16(x * x)
    return (_dot(hi, bd) + _dot(lo, bd)) * (1.0 / HALF)


def _rms_mod(x, g, sc, sh):
    ms = jnp.mean(x * x, axis=-1, keepdims=True)
    return x * lax.rsqrt(ms + EPS) * g * (1.0 + sc) + sh


def _rope(x, c, sg, lane):
    sw = jnp.where((lane % 32) < 16, pltpu.roll(x, LANES - 16, 1), pltpu.roll(x, 16, 1))
    return x * c + sw * sg


def _pair_lhs(qs, lane_row):
    ma = jnp.where(lane_row < HALF, 1.0, 0.0).astype(BF16)
    mb = jnp.where(lane_row < HALF, 0.0, 1.0).astype(BF16)
    return jnp.concatenate([qs * ma, qs * mb], axis=0)


def _mod_kernel(cc_ref, w_ref, b_ref, o_ref):
    cc = cc_ref[...]
    s = cc * jax.nn.sigmoid(cc)
    sh, sl = _split_bf16(s)
    wh, wl = _split_bf16(w_ref[0])
    o_ref[0] = _dot(sh, wh) + _dot(sl, wh) + _dot(sh, wl) + b_ref[0]


def _modulation(cc, ada_w, ada_b):
    tn = 1024
    return pl.pallas_call(
        _mod_kernel,
        grid=(DEPTH, 6 * D_MODEL // tn),
        in_specs=[pl.BlockSpec((8, D_MODEL), lambda l, j: (0, 0)),
                  pl.BlockSpec((1, D_MODEL, tn), lambda l, j: (l, 0, j)),
                  pl.BlockSpec((1, 1, tn), lambda l, j: (l, 0, j))],
        out_specs=pl.BlockSpec((1, 8, tn), lambda l, j: (l, 0, j)),
        out_shape=jax.ShapeDtypeStruct((DEPTH, 8, 6 * D_MODEL), F32),
        compiler_params=_cparams(2, 32),
        name="modulation",
    )(cc, ada_w, ada_b.reshape(DEPTH, 1, 6 * D_MODEL))


def _proj_even_kernel(x_ref, g_ref, sc_ref, sh_ref, w_ref, bd512_ref, bd128_ref, qg_ref, kg_ref, vg_ref,
                      cos_ref, sin_ref, ws_ref, bsb_ref, q_ref, kd_ref, v_ref, b_ref, *, ts):
    h = _rms_mod(x_ref[...], g_ref[...], sc_ref[...], sh_ref[...])
    p = _dot(h.astype(BF16), w_ref[...])
    lane = lax.broadcasted_iota(jnp.int32, (ts, LANES), 1)
    c = cos_ref[...]
    sg = sin_ref[...]

    q = p[:, :A_Q]
    qn = q * lax.rsqrt(_seg_mean_sq(q, bd512_ref[...]) + EPS) * qg_ref[...]
    for sl in range(A_Q // LANES):
        xs = qn[:, sl * LANES:(sl + 1) * LANES]
        q_ref[:, sl * LANES:(sl + 1) * LANES] = (_rope(xs, c, sg, lane) * QK_SCALE_LOG2).astype(BF16)

    k = p[:, A_Q:A_Q + A_KV]
    kn = k * lax.rsqrt(_seg_mean_sq(k, bd128_ref[...]) + EPS) * kg_ref[...]
    kt = _rope(kn, c, sg, lane).T
    for j in range(A_KV_HEADS):
        kj = kt[j * HALF:(j + 1) * HALF]
        kd_ref[j] = jnp.concatenate([kj, kj], axis=0).astype(BF16)
    v = p[:, A_Q + A_KV:A_IN]
    v_ref[0] = jnp.where(lane < HALF, v, 1.0).astype(BF16)
    v_ref[1] = jnp.where(lane < HALF, pltpu.roll(v, HALF, 1), 1.0).astype(BF16)

    zu = jax.nn.gelu(p[:, A_IN:A_IN + B_WIDTH])
    zv = jax.nn.gelu(p[:, A_IN + B_WIDTH:])
    vn = (zv * lax.rsqrt(_seg_mean_sq(zv, bd512_ref[...]) + EPS) * vg_ref[...]).astype(BF16)
    lane_c = lax.broadcasted_iota(jnp.int32, (CHUNK, LANES), 1)
    for ch in range(ts // CHUNK):
        rows = slice(ch * CHUNK, (ch + 1) * CHUNK)
        for sl in range(B_WIDTH // LANES):
            cols = slice(sl * LANES, (sl + 1) * LANES)
            vs = vn[rows, cols]
            mixed = jnp.where(lane_c < HALF, _dot(ws_ref[2 * sl], vs), _dot(ws_ref[2 * sl + 1], vs))
            b_ref[rows, cols] = (zu[rows, cols] * (mixed + bsb_ref[:, cols])).astype(BF16)


def _proj_even(x, mods, w_in, consts, cos, sin, ts):
    n = x.shape[0]
    g, sc, sh = mods
    bd512, bd128, qg, kg, vg, ws, bsb = consts
    row = lambda i: (i, 0)
    vec = lambda w: pl.BlockSpec((1, w), lambda i: (0, 0))
    return pl.pallas_call(
        functools.partial(_proj_even_kernel, ts=ts),
        grid=(n // ts,),
        in_specs=[pl.BlockSpec((ts, D_MODEL), row), vec(D_MODEL), vec(D_MODEL), vec(D_MODEL),
                  _full((D_MODEL, EVEN_IN)), _full((A_Q, A_Q)), _full((LANES, LANES)),
                  vec(A_Q), vec(A_KV), vec(B_WIDTH),
                  pl.BlockSpec((ts, LANES), row), pl.BlockSpec((ts, LANES), row),
                  _full((B_GROUPS, CHUNK, CHUNK)), _full((CHUNK, B_WIDTH))],
        out_specs=[pl.BlockSpec((ts, A_Q), row),
                   pl.BlockSpec((A_KV_HEADS, LANES, ts), lambda i: (0, 0, i)),
                   pl.BlockSpec((A_KV_HEADS, ts, LANES), lambda i: (0, i, 0)),
                   pl.BlockSpec((ts, B_WIDTH), row)],
        out_shape=[jax.ShapeDtypeStruct((n, A_Q), BF16),
                   jax.ShapeDtypeStruct((A_KV_HEADS, LANES, n), BF16),
                   jax.ShapeDtypeStruct((A_KV_HEADS, n, LANES), BF16),
                   jax.ShapeDtypeStruct((n, B_WIDTH), BF16)],
        compiler_params=_cparams(1, 56),
        name="proj_even",
    )(x, g, sc, sh, w_in, bd512, bd128, qg, kg, vg, cos, sin, ws, bsb)


def _attn_even_kernel(q_ref, kd_ref, va_ref, o_ref, lhs_scr, s_a, s_b, mx_a, mx_b, m_scr, acc_scr,
                      *, tq, tk, n_chunks):
    rows = 4 * tq
    lane_row = lax.broadcasted_iota(jnp.int32, (1, LANES), 1)
    lane = lax.broadcasted_iota(jnp.int32, (tq, LANES), 1)
    for j in range(A_KV_HEADS):
        lhs_scr[...] = jnp.concatenate([_pair_lhs(q_ref[:, sl * LANES:(sl + 1) * LANES], lane_row)
                                        for sl in (2 * j, 2 * j + 1)], axis=0)
        m_scr[...] = jnp.full((rows, LANES), NEG_BIG, F32)
        acc_scr[...] = jnp.zeros((rows, LANES), F32)

        def scores(c, s_scr, mx_scr, j=j):
            off = pl.multiple_of(c * tk, tk)
            s = _dot(lhs_scr[...], kd_ref[j, :, pl.ds(off, tk)])
            s_scr[...] = s
            mx_scr[...] = jnp.broadcast_to(jnp.max(s, axis=-1, keepdims=True), (rows, LANES))

        def accumulate(c, s_scr, mx_scr, j=j):
            off = pl.multiple_of(c * tk, tk)
            m_old = m_scr[...]
            m_new = jnp.maximum(m_old, mx_scr[...])
            p = jnp.concatenate([jnp.exp2(s_scr[:, t * LANES:(t + 1) * LANES] - m_new)
                                 for t in range(tk // LANES)], axis=1).astype(BF16)
            pv = _dot(p, va_ref[j, pl.ds(off, tk), :])
            acc_scr[...] = jnp.exp2(m_old - m_new) * acc_scr[...] + pv
            m_scr[...] = m_new

        scores(0, s_a, mx_a)

        def body(i, carry):
            scores(2 * i + 1, s_b, mx_b)
            accumulate(2 * i, s_a, mx_a)
            scores(2 * i + 2, s_a, mx_a)
            accumulate(2 * i + 1, s_b, mx_b)
            return carry

        lax.fori_loop(0, (n_chunks - 1) // 2, body, 0)
        accumulate(n_chunks - 1, s_a, mx_a)
        acc = acc_scr[...]
        o = acc / pltpu.roll(acc, HALF, 1)
        for t, sl in enumerate((2 * j, 2 * j + 1)):
            oa = o[(2 * t) * tq:(2 * t + 1) * tq]
            ob = o[(2 * t + 1) * tq:(2 * t + 2) * tq]
            o_ref[:, sl * LANES:(sl + 1) * LANES] = jnp.where(lane < HALF, oa, pltpu.roll(ob, HALF, 1)).astype(BF16)


def _attn_even(q, kd, va, tq, tk):
    n = q.shape[0]
    n_chunks = kd.shape[2] // tk
    assert n_chunks % 2 == 1
    rows = 4 * tq
    big = lambda w: pltpu.VMEM((rows, w), F32)
    return pl.pallas_call(
        functools.partial(_attn_even_kernel, tq=tq, tk=tk, n_chunks=n_chunks),
        grid=(n // tq,),
        in_specs=[pl.BlockSpec((tq, A_Q), lambda i: (i, 0)), _full(kd.shape), _full(va.shape)],
        out_specs=pl.BlockSpec((tq, A_Q), lambda i: (i, 0)),
        out_shape=jax.ShapeDtypeStruct((n, A_Q), BF16),
        scratch_shapes=[pltpu.VMEM((rows, LANES), BF16), big(tk), big(tk), big(LANES), big(LANES),
                        big(LANES), big(LANES)],
        compiler_params=_cparams(1, 56),
        name="attn_even",
    )(q, kd, va)


def _proj_odd_kernel(x_ref, g_ref, sc_ref, sh_ref, w_ref, bd512_ref, qg_ref, kg_ref,
                     q_ref, k_ref, v_ref, y_ref):
    h = _rms_mod(x_ref[...], g_ref[...], sc_ref[...], sh_ref[...])
    p = _dot(h.astype(BF16), w_ref[...])
    bd = bd512_ref[...]
    q = p[:, :C_W]
    q_ref[...] = (q * lax.rsqrt(_seg_mean_sq(q, bd) + EPS) * qg_ref[...] * (HEAD_DIM ** -0.5)).astype(BF16)
    k = p[:, C_W:2 * C_W]
    k_ref[...] = (k * lax.rsqrt(_seg_mean_sq(k, bd) + EPS) * kg_ref[...]).astype(BF16)
    v_ref[...] = p[:, 2 * C_W:3 * C_W].astype(BF16)
    y_ref[...] = p[:, 3 * C_W:3 * C_W + D_CH] * jax.nn.sigmoid(p[:, 3 * C_W + D_CH:])


def _proj_odd(x, mods, w_in, consts, ts):
    n = x.shape[0]
    g, sc, sh = mods
    bd512, qg, kg = consts
    row = lambda i: (i, 0)
    vec = lambda w: pl.BlockSpec((1, w), lambda i: (0, 0))
    blk = pl.BlockSpec((ts, C_W), row)
    return pl.pallas_call(
        _proj_odd_kernel,
        grid=(n // ts,),
        in_specs=[pl.BlockSpec((ts, D_MODEL), row), vec(D_MODEL), vec(D_MODEL), vec(D_MODEL),
                  _full((D_MODEL, ODD_IN)), _full((C_W, C_W)), vec(C_W), vec(C_W)],
        out_specs=[blk, blk, blk, blk],
        out_shape=[jax.ShapeDtypeStruct((n, C_W), BF16), jax.ShapeDtypeStruct((n, C_W), BF16),
                   jax.ShapeDtypeStruct((n, C_W), BF16), jax.ShapeDtypeStruct((n, D_CH), F32)],
        compiler_params=_cparams(1, 56),
        name="proj_odd",
    )(x, g, sc, sh, w_in, bd512, qg, kg)


NA_BLOCK_ROWS = 8
NA_TOK = NA_BLOCK_ROWS * GRID_W
NA_WIN = NA_ROWS * GRID_W


def _na_kernel(q_ref, kp_ref, kc_ref, kn_ref, vp_ref, vc_ref, vn_ref, kx_ref, vx_ref, bias_ref, o_ref,
               kbuf, vbuf, *, rows):
    i = pl.program_id(0)
    for t, (kr, vr) in enumerate(((kp_ref, vp_ref), (kc_ref, vc_ref), (kn_ref, vn_ref))):
        kbuf[t * NA_TOK:(t + 1) * NA_TOK] = kr[...]
        vbuf[t * NA_TOK:(t + 1) * NA_TOK] = vr[...]
    lane_row = lax.broadcasted_iota(jnp.int32, (1, LANES), 1)
    lane = lax.broadcasted_iota(jnp.int32, (GRID_W, LANES), 1)
    for sl in range(C_W // LANES):
        cols = slice(sl * LANES, (sl + 1) * LANES)
        lhs = _pair_lhs(q_ref[:, cols], lane_row)
        s_ctx = _dot_nt(lhs, kx_ref[:, cols])
        sa, sb, starts = [], [], []
        for j in range(NA_BLOCK_ROWS):
            r = i * NA_BLOCK_ROWS + j
            r0 = jnp.clip(r - NA_ROWS // 2, 0, rows - NA_ROWS)
            start = pl.multiple_of((r0 - (i - 1) * NA_BLOCK_ROWS) * GRID_W, GRID_W)
            starts.append(start)
            qa = lhs[j * GRID_W:(j + 1) * GRID_W]
            qb = lhs[NA_TOK + j * GRID_W:NA_TOK + (j + 1) * GRID_W]
            kw = kbuf[pl.ds(start, NA_WIN), cols]
            s = _dot_nt(jnp.concatenate([qa, qb], axis=0), kw) + bias_ref[r0 - r + NA_ROWS - 1, sl]
            sa.append(s[:GRID_W])
            sb.append(s[GRID_W:])
        s_loc = jnp.concatenate(sa + sb, axis=0)
        m = jnp.maximum(jnp.max(s_loc, axis=-1, keepdims=True), jnp.max(s_ctx, axis=-1, keepdims=True))
        p_loc = jnp.exp(s_loc - m)
        p_ctx = jnp.exp(s_ctx - m)
        l = jnp.sum(p_loc, axis=-1, keepdims=True) + jnp.sum(p_ctx, axis=-1, keepdims=True)
        o_ctx = _dot(p_ctx.astype(BF16), vx_ref[:, cols])
        p_loc = p_loc.astype(BF16)
        for j in range(NA_BLOCK_ROWS):
            ra = slice(j * GRID_W, (j + 1) * GRID_W)
            rb = slice(NA_TOK + j * GRID_W, NA_TOK + (j + 1) * GRID_W)
            vw = vbuf[pl.ds(starts[j], NA_WIN), cols]
            o2 = _dot(jnp.concatenate([p_loc[ra], p_loc[rb]], axis=0), vw)
            oa = (o2[:GRID_W] + o_ctx[ra]) / l[ra]
            ob = (o2[GRID_W:] + o_ctx[rb]) / l[rb]
            o_ref[ra, cols] = jnp.where(lane < HALF, oa, ob).astype(BF16)


def _na_attention(q, k, v, kx, vx, bias):
    n = q.shape[0]
    nb = n // NA_TOK
    blk = lambda f: pl.BlockSpec((NA_TOK, C_W), f)
    prev = lambda i: (jnp.maximum(i - 1, 0), 0)
    cur = lambda i: (i, 0)
    nxt = lambda i: (jnp.minimum(i + 1, nb - 1), 0)
    return pl.pallas_call(
        functools.partial(_na_kernel, rows=n // GRID_W),
        grid=(nb,),
        in_specs=[blk(cur), blk(prev), blk(cur), blk(nxt), blk(prev), blk(cur), blk(nxt),
                  _full(kx.shape), _full(vx.shape), _full(bias.shape)],
        out_specs=blk(cur),
        out_shape=jax.ShapeDtypeStruct((n, C_W), BF16),
        scratch_shapes=[pltpu.VMEM((3 * NA_TOK, C_W), BF16), pltpu.VMEM((3 * NA_TOK, C_W), BF16)],
        compiler_params=_cparams(1, 56),
        name="na_attention",
    )(q, k, k, k, v, v, v, kx, vx, bias)


def _na_bias(rpb):
    cc = np.arange(GRID_W)
    c0 = np.clip(cc - NA_COLS // 2, 0, GRID_W - NA_COLS)
    kc = np.arange(GRID_W)
    inwin = (kc[None, :] >= c0[:, None]) & (kc[None, :] < c0[:, None] + NA_COLS)
    dc = np.clip(kc[None, :] - cc[:, None] + NA_COLS - 1, 0, 2 * NA_COLS - 2)
    dr = np.arange(NA_ROWS)[:, None] + np.arange(NA_ROWS)[None, :]
    t = rpb.astype(F32)[:, dr][:, :, :, dc]
    t = jnp.where(jnp.asarray(inwin)[None, None, None], t, NEG_BIG)
    t = jnp.transpose(t, (1, 0, 3, 2, 4))
    return t.reshape(NA_ROWS, C_HEADS // 2, 2 * GRID_W, NA_WIN)


def _mha_kernel(q_ref, k_ref, v_ref, o_ref):
    n = q_ref.shape[0]
    lane_row = lax.broadcasted_iota(jnp.int32, (1, LANES), 1)
    lane = lax.broadcasted_iota(jnp.int32, (n, LANES), 1)
    for sl in range(C_W // LANES):
        cols = slice(sl * LANES, (sl + 1) * LANES)
        s = _dot_nt(_pair_lhs(q_ref[:, cols], lane_row), k_ref[:, cols])
        p = jnp.exp(s - jnp.max(s, axis=-1, keepdims=True))
        l = jnp.sum(p, axis=-1, keepdims=True)
        o = _dot(p.astype(BF16), v_ref[:, cols]) / l
        o_ref[:, cols] = jnp.where(lane < HALF, o[:n], o[n:]).astype(BF16)


def _mha_small(q, k, v):
    return pl.pallas_call(
        _mha_kernel,
        grid=(1,),
        in_specs=[_full(q.shape), _full(k.shape), _full(v.shape)],
        out_specs=_full(q.shape),
        out_shape=jax.ShapeDtypeStruct(q.shape, BF16),
        compiler_params=_cparams(1, 32),
        name="mha_small",
    )(q, k, v)


def _conv_kernel(y_ref, yp_ref, yn_ref, dw_ref, dwb_ref, lng_ref, lnb_ref, o_ref, ext, *, ts):
    i = pl.program_id(0)
    nb = pl.num_programs(0)
    ext[0:CONV_HALO] = jnp.where(i > 0, yp_ref[...], 0.0)
    ext[CONV_HALO:CONV_HALO + ts] = y_ref[...]
    ext[CONV_HALO + ts:] = jnp.where(i < nb - 1, yn_ref[...], 0.0)
    acc = jnp.zeros((ts, D_CH), F32)
    base = CONV_HALO - CONV_W // 2
    for j in range(CONV_W):
        acc = acc + ext[base + j:base + j + ts, :] * dw_ref[j:j + 1, :]
    yb = acc + dwb_ref[...]
    mu = jnp.mean(yb, axis=-1, keepdims=True)
    var = jnp.mean(jnp.square(yb - mu), axis=-1, keepdims=True)
    z = (yb - mu) * lax.rsqrt(var + EPS) * lng_ref[...] + lnb_ref[...]
    o_ref[...] = (z * jax.nn.sigmoid(z)).astype(BF16)


def _conv_module(y, dw, dwb, lng, lnb, ts):
    n = y.shape[0]
    hb = ts // CONV_HALO
    nh = n // CONV_HALO
    vec = pl.BlockSpec((1, D_CH), lambda i: (0, 0))
    return pl.pallas_call(
        functools.partial(_conv_kernel, ts=ts),
        grid=(n // ts,),
        in_specs=[pl.BlockSpec((ts, D_CH), lambda i: (i, 0)),
                  pl.BlockSpec((CONV_HALO, D_CH), lambda i: (jnp.maximum(i * hb - 1, 0), 0)),
                  pl.BlockSpec((CONV_HALO, D_CH), lambda i: (jnp.minimum((i + 1) * hb, nh - 1), 0)),
                  _full((CONV_W + 1, D_CH)), vec, vec, vec],
        out_specs=pl.BlockSpec((ts, D_CH), lambda i: (i, 0)),
        out_shape=jax.ShapeDtypeStruct((n, D_CH), BF16),
        scratch_shapes=[pltpu.VMEM((ts + 2 * CONV_HALO, D_CH), F32)],
        compiler_params=_cparams(1, 32),
        name="conv_module",
    )(y, y, y, dw, dwb, lng, lnb)


def _route(logits, rb, lane):
    aff = jax.nn.sigmoid(logits)
    sel = aff + rb
    big = 4 * LANES

    def top2(valid):
        v = jnp.where(valid, sel, -jnp.inf)
        m1 = jnp.max(v, axis=-1, keepdims=True)
        i1 = jnp.min(jnp.where(v == m1, lane, big), axis=-1, keepdims=True)
        v2 = jnp.where(lane == i1, -jnp.inf, v)
        m2 = jnp.max(v2, axis=-1, keepdims=True)
        i2 = jnp.min(jnp.where(v2 == m2, lane, big), axis=-1, keepdims=True)
        return m1, m2, i1, i2

    best = None
    for g in range(N_GROUPS):
        m1, m2, _, _ = top2((lane >= g * EXPERTS_PER_GROUP) & (lane < (g + 1) * EXPERTS_PER_GROUP))
        score = m1 + m2
        if best is None:
            best, bg = score, jnp.zeros_like(score, dtype=jnp.int32)
        else:
            upd = score > best
            best = jnp.where(upd, score, best)
            bg = jnp.where(upd, g, bg)
    lo = bg * EXPERTS_PER_GROUP
    _, _, i1, i2 = top2((lane >= lo) & (lane < lo + EXPERTS_PER_GROUP))
    a1 = jnp.sum(jnp.where(lane == i1, aff, 0.0), axis=-1, keepdims=True)
    a2 = jnp.sum(jnp.where(lane == i2, aff, 0.0), axis=-1, keepdims=True)
    tot = a1 + a2
    out = jnp.where(lane == 0, i1.astype(F32), 0.0)
    out = jnp.where(lane == 1, i2.astype(F32), out)
    out = jnp.where(lane == 2, a1 / tot, out)
    return jnp.where(lane == 3, a2 / tot, out)


def _post_mix_kernel(x_ref, a_ref, b_ref, w_ref, g1_ref, ng_ref, sc_ref, sh_ref, rwh_ref, rwl_ref, rb_ref,
                     xo_ref, h2_ref, route_ref, *, ts):
    half = w_ref.shape[0] // 2
    mo = _dot(a_ref[...], w_ref[:half]) + _dot(b_ref[...], w_ref[half:])
    xn = x_ref[...] + g1_ref[...] * mo
    xo_ref[...] = xn
    h2 = _rms_mod(xn, ng_ref[...], sc_ref[...], sh_ref[...])
    h2_ref[...] = h2
    hh, hl = _split_bf16(h2)
    logits = _dot(hh, rwh_ref[...]) + _dot(hl, rwh_ref[...]) + _dot(hh, rwl_ref[...])
    lane = lax.broadcasted_iota(jnp.int32, (ts, LANES), 1)
    route_ref[...] = _route(logits, rb_ref[...], lane)


def _post_mix(x, a, b, w_out, mods, router, ts):
    n = x.shape[0]
    g1, ng, sc, sh = mods
    rwh, rwl, rb = router
    row = lambda i: (i, 0)
    vec = lambda w: pl.BlockSpec((1, w), lambda i: (0, 0))
    return pl.pallas_call(
        functools.partial(_post_mix_kernel, ts=ts),
        grid=(n // ts,),
        in_specs=[pl.BlockSpec((ts, D_MODEL), row), pl.BlockSpec((ts, a.shape[1]), row),
                  pl.BlockSpec((ts, b.shape[1]), row), _full(w_out.shape),
                  vec(D_MODEL), vec(D_MODEL), vec(D_MODEL), vec(D_MODEL),
                  _full(rwh.shape), _full(rwl.shape), vec(LANES)],
        out_specs=[pl.BlockSpec((ts, D_MODEL), row), pl.BlockSpec((ts, D_MODEL), row),
                   pl.BlockSpec((ts, LANES), row)],
        out_shape=[jax.ShapeDtypeStruct((n, D_MODEL), F32), jax.ShapeDtypeStruct((n, D_MODEL), F32),
                   jax.ShapeDtypeStruct((n, LANES), F32)],
        compiler_params=_cparams(1, 56),
        name="post_mix",
    )(x, a, b, w_out, g1, ng, sc, sh, rwh, rwl, rb)


def _dispatch_kernel(dest_ref, h_ref, buf_in_ref, buf_ref, sem, *, ts):
    del buf_in_ref

    def row_copy(r, k):
        return pltpu.make_async_copy(h_ref.at[pl.ds(r, 1)], buf_ref.at[pl.ds(dest_ref[0, 0, 2 * r + k], 1)], sem)

    def issue(r, carry):
        row_copy(r, 0).start()
        row_copy(r, 1).start()
        return carry

    def drain(r, carry):
        row_copy(r, 0).wait()
        row_copy(r, 1).wait()
        return carry

    lax.fori_loop(0, ts, issue, 0, unroll=DMA_UNROLL)
    lax.fori_loop(0, ts, drain, 0, unroll=DMA_UNROLL)


def _dispatch(h2, dest3, n_rows, ts):
    n = h2.shape[0]
    buf0 = jnp.zeros((n_rows, D_MODEL), F32)
    return pl.pallas_call(
        functools.partial(_dispatch_kernel, ts=ts),
        grid=(n // ts,),
        in_specs=[pl.BlockSpec((1, 1, 2 * ts), lambda i: (i, 0, 0), memory_space=pltpu.SMEM),
                  pl.BlockSpec((ts, D_MODEL), lambda i: (i, 0)),
                  pl.BlockSpec(memory_space=pl.ANY)],
        out_specs=pl.BlockSpec(memory_space=pl.ANY),
        out_shape=jax.ShapeDtypeStruct((n_rows, D_MODEL), F32),
        scratch_shapes=[pltpu.SemaphoreType.DMA(())],
        input_output_aliases={2: 0},
        compiler_params=_cparams(1, 32),
        name="moe_dispatch",
    )(dest3, h2, buf0)


def _expert_kernel(blk_e_ref, n_used_ref, x_ref, wg_ref, wu_ref, wd_ref, y_ref):
    del blk_e_ref
    i = pl.program_id(0)

    @pl.when(i < n_used_ref[0])
    def _():
        xb = x_ref[...].astype(BF16)
        gate = _dot(xb, wg_ref[0])
        up = _dot(xb, wu_ref[0])
        hmid = gate * jax.nn.sigmoid(gate) * up
        y_ref[...] = _dot(hmid.astype(BF16), wd_ref[0])

    @pl.when(i >= n_used_ref[0])
    def _():
        y_ref[...] = jnp.zeros_like(y_ref)


def _experts(buf, blk_e, n_used, wg, wu, wd):
    n_rows = buf.shape[0]
    nb = n_rows // MOE_BLOCK
    grid_spec = pltpu.PrefetchScalarGridSpec(
        num_scalar_prefetch=2,
        grid=(nb,),
        in_specs=[pl.BlockSpec((MOE_BLOCK, D_MODEL), lambda i, be, nu: (i, 0)),
                  pl.BlockSpec((1, D_MODEL, D_EXPERT), lambda i, be, nu: (be[i], 0, 0)),
                  pl.BlockSpec((1, D_MODEL, D_EXPERT), lambda i, be, nu: (be[i], 0, 0)),
                  pl.BlockSpec((1, D_EXPERT, D_MODEL), lambda i, be, nu: (be[i], 0, 0))],
        out_specs=pl.BlockSpec((MOE_BLOCK, D_MODEL), lambda i, be, nu: (i, 0)),
    )
    return pl.pallas_call(
        _expert_kernel,
        grid_spec=grid_spec,
        out_shape=jax.ShapeDtypeStruct((n_rows, D_MODEL), F32),
        compiler_params=_cparams(1, 48),
        name="moe_experts",
    )(blk_e, n_used, buf, wg, wu, wd)


def _combine_kernel(dest_ref, x_ref, route_ref, g2_ref, y_ref, o_ref, g0, g1, sem, *, ts):
    def row_copy(r, k):
        dst = g0 if k == 0 else g1
        return pltpu.make_async_copy(y_ref.at[pl.ds(dest_ref[0, 0, 2 * r + k], 1)], dst.at[pl.ds(r, 1)], sem)

    def issue(r, carry):
        row_copy(r, 0).start()
        row_copy(r, 1).start()
        return carry

    def drain(r, carry):
        row_copy(r, 0).wait()
        row_copy(r, 1).wait()
        return carry

    lax.fori_loop(0, ts, issue, 0, unroll=DMA_UNROLL)
    lax.fori_loop(0, ts, drain, 0, unroll=DMA_UNROLL)
    route = route_ref[...]
    f = g0[...] * route[:, 2:3] + g1[...] * route[:, 3:4]
    o_ref[...] = x_ref[...] + g2_ref[...] * f


def _combine(x, route, g2, y, dest3, ts):
    n = x.shape[0]
    row = lambda i: (i, 0)
    return pl.pallas_call(
        functools.partial(_combine_kernel, ts=ts),
        grid=(n // ts,),
        in_specs=[pl.BlockSpec((1, 1, 2 * ts), lambda i: (i, 0, 0), memory_space=pltpu.SMEM),
                  pl.BlockSpec((ts, D_MODEL), row), pl.BlockSpec((ts, LANES), row),
                  pl.BlockSpec((1, D_MODEL), lambda i: (0, 0)),
                  pl.BlockSpec(memory_space=pl.ANY)],
        out_specs=pl.BlockSpec((ts, D_MODEL), row),
        out_shape=jax.ShapeDtypeStruct((n, D_MODEL), F32),
        scratch_shapes=[pltpu.VMEM((ts, D_MODEL), F32), pltpu.VMEM((ts, D_MODEL), F32),
                        pltpu.SemaphoreType.DMA(())],
        compiler_params=_cparams(1, 48),
        name="moe_combine",
    )(dest3, x, route, g2, y)


def _moe(x, h2, route, g2, wg, wu, wd, ts):
    n = x.shape[0]
    e_flat = route[:, :2].astype(jnp.int32).reshape(2 * n)
    onehot = (e_flat[:, None] == jnp.arange(N_EXPERTS, dtype=jnp.int32)[None, :]).astype(jnp.int32)
    csum = jnp.cumsum(onehot, axis=0)
    rank = jnp.sum((csum - onehot) * onehot, axis=1)
    counts = csum[-1]
    pcounts = (counts + MOE_BLOCK - 1) // MOE_BLOCK * MOE_BLOCK
    pend = jnp.cumsum(pcounts)
    pstart = pend - pcounts
    dest = jnp.sum(onehot * pstart[None, :], axis=1) + rank
    n_blocks = -(-2 * n // MOE_BLOCK) + N_EXPERTS
    blk_pos = jnp.arange(n_blocks, dtype=jnp.int32) * MOE_BLOCK
    blk_e = jnp.minimum(jnp.sum((pend[None, :] <= blk_pos[:, None]).astype(jnp.int32), axis=1), N_EXPERTS - 1)
    n_used = (pend[-1:] // MOE_BLOCK).astype(jnp.int32)
    dest3 = dest.astype(jnp.int32).reshape(n // ts, 1, 2 * ts)
    buf = _dispatch(h2, dest3, n_blocks * MOE_BLOCK, ts)
    y = _experts(buf, blk_e, n_used, wg, wu, wd)
    return _combine(x, route, g2, y, dest3, ts)


def _block_diag_ones(width):
    idx = np.arange(width) // HALF
    return jnp.asarray((idx[:, None] == idx[None, :]).astype(np.float32), dtype=BF16)


def _rope_tables(n_tok):
    t = jnp.arange(n_tok, dtype=jnp.int32)
    pos = jnp.stack([t // GRID_W, t % GRID_W], axis=-1).astype(F32)
    inv = ROPE_THETA ** (-jnp.arange(ROPE_FREQS, dtype=F32) / ROPE_FREQS)
    ang = pos[:, :, None] * inv
    cos, sin = jnp.cos(ang), jnp.sin(ang)
    c = jnp.concatenate([cos, cos], axis=-1).reshape(n_tok, HEAD_DIM)
    s = jnp.concatenate([-sin, sin], axis=-1).reshape(n_tok, HEAD_DIM)
    return jnp.tile(c, (1, 2)), jnp.tile(s, (1, 2))


def _tile_vec(v, reps):
    return jnp.tile(v.astype(F32), reps).reshape(1, -1)


def kernel(x, c, ctx, c_ctx, router_w, router_b, ada_w, ada_b, norm1_g, norm2_g, ev_w_in, ev_w_out, a_q_norm,
           a_k_norm, b_v_norm, b_ws, b_bs, od_w_in, od_w_out, c_q_norm, c_k_norm, c_rpb, d_dw, d_dw_b, d_ln_g,
           d_ln_b, moe_w_gate, moe_w_up, moe_w_down):
    assert x.shape[0] == 1 and ctx.shape[0] == 1
    s_len, n_ctx = x.shape[1], ctx.shape[1]
    ts_lat = min(512, s_len)
    ts_ctx = n_ctx
    x_lat = x[0]
    x_ctx = ctx[0]

    cc = jnp.zeros((8, D_MODEL), F32).at[0].set(c[0]).at[1].set(c_ctx)
    mods = _modulation(cc, ada_w, ada_b)

    def mod(l, row, k):
        return mods[l, row, k * D_MODEL:(k + 1) * D_MODEL].reshape(1, D_MODEL)

    bd512 = _block_diag_ones(A_Q)
    bd128 = _block_diag_ones(LANES)
    cos_lat, sin_lat = _rope_tables(s_len)
    cos_ctx, sin_ctx = jnp.ones((n_ctx, LANES), F32), jnp.zeros((n_ctx, LANES), F32)
    rw = jnp.zeros((D_MODEL, LANES), F32).at[:, :N_EXPERTS].set(router_w)
    rwh = rw.astype(BF16)
    rwl = (rw - rwh.astype(F32)).astype(BF16)
    rb = jnp.zeros((1, LANES), F32).at[0, :N_EXPERTS].set(router_b)
    router = (rwh, rwl, rb)

    for l in range(DEPTH):
        need_ctx = l < DEPTH - 1
        i = l // 2
        g_n1 = norm1_g[l].reshape(1, D_MODEL)
        g_n2 = norm2_g[l].reshape(1, D_MODEL)
        m_lat = [mod(l, 0, k) for k in range(6)]
        m_ctx = [mod(l, 1, k) for k in range(6)]
        if l % 2 == 0:
            w_in = ev_w_in[i].astype(BF16)
            w_out = ev_w_out[i].astype(BF16)
            consts = (bd512, bd128, _tile_vec(a_q_norm[i], A_HEADS), _tile_vec(a_k_norm[i], A_KV_HEADS),
                      b_v_norm[i].reshape(1, B_WIDTH).astype(F32), b_ws[i].astype(BF16),
                      jnp.repeat(b_bs[i].T, HALF, axis=1))
            q_l, kd_l, v_l, b_l = _proj_even(x_lat, (g_n1, m_lat[1], m_lat[0]), w_in, consts, cos_lat, sin_lat, ts_lat)
            q_c, kd_c, v_c, b_c = _proj_even(x_ctx, (g_n1, m_ctx[1], m_ctx[0]), w_in, consts, cos_ctx, sin_ctx, ts_ctx)
            kd_cc = jnp.concatenate([kd_c, kd_c], axis=2)
            v_cc = jnp.concatenate([v_c, jnp.zeros_like(v_c)], axis=1)
            kd_all = jnp.concatenate([kd_l, kd_cc], axis=2)
            v_all = jnp.concatenate([v_l, v_cc], axis=1)
            mix_l = (_attn_even(q_l, kd_all, v_all, min(ATTN_TQ, s_len), ATTN_TK), b_l)
            if need_ctx:
                mix_c = (_attn_even(q_c, kd_cc, v_cc, ATTN_TQ, ATTN_TK), b_c)
        else:
            w_in = od_w_in[i].astype(BF16)
            w_out = od_w_out[i].astype(BF16)
            consts = (bd512, _tile_vec(c_q_norm[i], C_HEADS), _tile_vec(c_k_norm[i], C_HEADS))
            q_l, k_l, v_l, y_l = _proj_odd(x_lat, (g_n1, m_lat[1], m_lat[0]), w_in, consts, ts_lat)
            q_c, k_c, v_c, y_c = _proj_odd(x_ctx, (g_n1, m_ctx[1], m_ctx[0]), w_in, consts, ts_ctx)
            dw = jnp.zeros((CONV_W + 1, D_CH), F32).at[:CONV_W].set(d_dw[i])
            conv_p = (dw, d_dw_b[i].reshape(1, D_CH), d_ln_g[i].reshape(1, D_CH), d_ln_b[i].reshape(1, D_CH))
            mix_l = (_na_attention(q_l, k_l, v_l, k_c, v_c, _na_bias(c_rpb[i])), _conv_module(y_l, *conv_p, ts_lat))
            if need_ctx:
                mix_c = (_mha_small(q_c, k_c, v_c), _conv_module(y_c, *conv_p, ts_ctx))
        wg = moe_w_gate[l].astype(BF16)
        wu = moe_w_up[l].astype(BF16)
        wd = moe_w_down[l].astype(BF16)
        xn, h2, route = _post_mix(x_lat, *mix_l, w_out, (m_lat[2], g_n2, m_lat[4], m_lat[3]), router, ts_lat)
        x_lat = _moe(xn, h2, route, m_lat[5], wg, wu, wd, ts_lat)
        if need_ctx:
            xn, h2, route = _post_mix(x_ctx, *mix_c, w_out, (m_ctx[2], g_n2, m_ctx[4], m_ctx[3]), router, ts_ctx)
            x_ctx = _moe(xn, h2, route, m_ctx[5], wg, wu, wd, ts_ctx)
    return x_lat[None]
```

```python
import functools

import numpy as np
import jax
import jax.numpy as jnp
from jax import lax
from jax.experimental import pallas as pl
from jax.experimental.pallas import tpu as pltpu

F32 = jnp.float32
BF16 = jnp.bfloat16

D_MODEL = 1024
DEPTH = 4
GRID_W = 64
HEAD_DIM = 64
EPS = 1e-6
A_HEADS = 8
A_KV_HEADS = 2
A_Q = A_HEADS * HEAD_DIM
A_KV = A_KV_HEADS * HEAD_DIM
A_IN = A_Q + 2 * A_KV
ROPE_THETA = 10000.0
ROPE_FREQS = HEAD_DIM // 4
B_GROUPS = 8
B_WIDTH = 512
CHUNK = 128
C_HEADS = 8
C_W = C_HEADS * HEAD_DIM
NA_ROWS = 8
NA_COLS = 16
D_CH = 512
CONV_W = 31
EVEN_IN = A_IN + 2 * B_WIDTH
ODD_IN = 3 * C_W + 2 * D_CH
N_EXPERTS = 16
N_GROUPS = 4
EXPERTS_PER_GROUP = 4
D_EXPERT = 512
MOE_BLOCK = 256

LANES = 128
HALF = 64
NEG_BIG = -1e30
CONV_HALO = 16
DMA_UNROLL = 8
ATTN_TQ = 256
ATTN_TK = 512
QK_SCALE_LOG2 =HEAD_DIM ** -0.5 * 1.4426950408889634


def _cparams(n_axes, vmem_mb):
    return pltpu.CompilerParams(dimension_semantics=("arbitrary",) * n_axes,
                                vmem_limit_bytes=vmem_mb << 20)


def _full(shape):
    nd = len(shape)
    return pl.BlockSpec(shape, lambda *_: (0,) * nd)


def _dot(a, b):
    return jnp.dot(a, b, preferred_element_type=F32)


def _dot_nt(a, b):
    return lax.dot_general(a, b, (((1,), (1,)), ((), ())), preferred_element_type=F32)


def _split_bf16(x):
    hi = x.astype(BF16)
    lo = (x - hi.astype(F32)).astype(BF16)
    return hi, lo


def _seg_mean_sq(x, bd):
    hi, lo = _split_bf16(x * x)
    return (_dot(hi, bd) + _dot(lo, bd)) * (1.0 / HALF)


def _rms_mod(x, g, sc, sh):
    ms = jnp.mean(x * x, axis=-1, keepdims=True)
    return x * lax.rsqrt(ms + EPS) * g * (1.0 + sc) + sh


def _rope(x, c, sg, lane):
    sw = jnp.where((lane % 32) < 16, pltpu.roll(x, LANES - 16, 1), pltpu.roll(x, 16, 1))
    return x * c + sw * sg


def _pair_lhs(qs, lane_row):
    ma = jnp.where(lane_row < HALF, 1.0, 0.0).astype(BF16)
    mb = jnp.where(lane_row < HALF, 0.0, 1.0).astype(BF16)
    return jnp.concatenate([qs * ma, qs * mb], axis=0)


def _mod_kernel(cc_ref, w_ref, b_ref, o_ref):
    cc = cc_ref[...]
    s = cc * jax.nn.sigmoid(cc)
    sh, sl = _split_bf16(s)
    wh, wl = _split_bf16(w_ref[0])
    o_ref[0] = _dot(sh, wh) + _dot(sl, wh) + _dot(sh, wl) + b_ref[0]


def _modulation(cc, ada_w, ada_b):
    tn = 1024
    return pl.pallas_call(
        _mod_kernel,
        grid=(DEPTH, 6 * D_MODEL // tn),
        in_specs=[pl.BlockSpec((8, D_MODEL), lambda l, j: (0, 0)),
                  pl.BlockSpec((1, D_MODEL, tn), lambda l, j: (l, 0, j)),
                  pl.BlockSpec((1, 1, tn), lambda l, j: (l, 0, j))],
        out_specs=pl.BlockSpec((1, 8, tn), lambda l, j: (l, 0, j)),
        out_shape=jax.ShapeDtypeStruct((DEPTH, 8, 6 * D_MODEL), F32),
        compiler_params=_cparams(2, 32),
        name="modulation",
    )(cc, ada_w, ada_b.reshape(DEPTH, 1, 6 * D_MODEL))


def _proj_even_kernel(x_ref, g_ref, sc_ref, sh_ref, w_ref, bd512_ref, bd128_ref, qg_ref, kg_ref, vg_ref,
                      cos_ref, sin_ref, ws_ref, bsb_ref, q_ref, kd_ref, v_ref, b_ref, *, ts):
    h = _rms_mod(x_ref[...], g_ref[...], sc_ref[...], sh_ref[...])
    p = _dot(h.astype(BF16), w_ref[...])
    lane = lax.broadcasted_iota(jnp.int32, (ts, LANES), 1)
    c = cos_ref[...]
    sg = sin_ref[...]

    q = p[:, :A_Q]
    qn = q * lax.rsqrt(_seg_mean_sq(q, bd512_ref[...]) + EPS) * qg_ref[...]
    for sl in range(A_Q // LANES):
        xs = qn[:, sl * LANES:(sl + 1) * LANES]
        q_ref[:, sl * LANES:(sl + 1) * LANES] = (_rope(xs, c, sg, lane) * QK_SCALE_LOG2).astype(BF16)

    k = p[:, A_Q:A_Q + A_KV]
    kn = k * lax.rsqrt(_seg_mean_sq(k, bd128_ref[...]) + EPS) * kg_ref[...]
    kt = _rope(kn, c, sg, lane).T
    for j in range(A_KV_HEADS):
        kj = kt[j * HALF:(j + 1) * HALF]
        kd_ref[j] = jnp.concatenate([kj, kj], axis=0).astype(BF16)
    v = p[:, A_Q + A_KV:A_IN]
    v_ref[0] = jnp.where(lane < HALF, v, 1.0).astype(BF16)
    v_ref[1] = jnp.where(lane < HALF, pltpu.roll(v, HALF, 1), 1.0).astype(BF16)

    zu = jax.nn.gelu(p[:, A_IN:A_IN + B_WIDTH])
    zv = jax.nn.gelu(p[:, A_IN + B_WIDTH:])
    vn = (zv * lax.rsqrt(_seg_mean_sq(zv, bd512_ref[...]) + EPS) * vg_ref[...]).astype(BF16)
    lane_c = lax.broadcasted_iota(jnp.int32, (CHUNK, LANES), 1)
    for ch in range(ts // CHUNK):
        rows = slice(ch * CHUNK, (ch + 1) * CHUNK)
        for sl in range(B_WIDTH // LANES):
            cols = slice(sl * LANES, (sl + 1) * LANES)
            vs = vn[rows, cols]
            mixed = jnp.where(lane_c < HALF, _dot(ws_ref[2 * sl], vs), _dot(ws_ref[2 * sl + 1], vs))
            b_ref[rows, cols] = (zu[rows, cols] * (mixed + bsb_ref[:, cols])).astype(BF16)


def _proj_even(x, mods, w_in, consts, cos, sin, ts):
    n = x.shape[0]
    g, sc, sh = mods
    bd512, bd128, qg, kg, vg, ws, bsb = consts
    row = lambda i: (i, 0)
    vec = lambda w: pl.BlockSpec((1, w), lambda i: (0, 0))
    return pl.pallas_call(
        functools.partial(_proj_even_kernel, ts=ts),
        grid=(n // ts,),
        in_specs=[pl.BlockSpec((ts, D_MODEL), row), vec(D_MODEL), vec(D_MODEL), vec(D_MODEL),
                  _full((D_MODEL, EVEN_IN)), _full((A_Q, A_Q)), _full((LANES, LANES)),
                  vec(A_Q), vec(A_KV), vec(B_WIDTH),
                  pl.BlockSpec((ts, LANES), row), pl.BlockSpec((ts, LANES), row),
                  _full((B_GROUPS, CHUNK, CHUNK)), _full((CHUNK, B_WIDTH))],
        out_specs=[pl.BlockSpec((ts, A_Q), row),
                   pl.BlockSpec((A_KV_HEADS, LANES, ts), lambda i: (0, 0, i)),
                   pl.BlockSpec((A_KV_HEADS, ts, LANES), lambda i: (0, i, 0)),
                   pl.BlockSpec((ts, B_WIDTH), row)],
        out_shape=[jax.ShapeDtypeStruct((n, A_Q), BF16),
                   jax.ShapeDtypeStruct((A_KV_HEADS, LANES, n), BF16),
                   jax.ShapeDtypeStruct((A_KV_HEADS, n, LANES), BF16),
                   jax.ShapeDtypeStruct((n, B_WIDTH), BF16)],
        compiler_params=_cparams(1, 56),
        name="proj_even",
    )(x, g, sc, sh, w_in, bd512, bd128, qg, kg, vg, cos, sin, ws, bsb)


def _attn_even_kernel(q_ref, kd_ref, va_ref, o_ref, lhs_scr, s_a, s_b, mx_a, mx_b, m_scr, acc_scr,
                      *, tq, tk, n_chunks):
    rows = 4 * tq
    lane_row = lax.broadcasted_iota(jnp.int32, (1, LANES), 1)
    lane = lax.broadcasted_iota(jnp.int32, (tq, LANES), 1)
    for j in range(A_KV_HEADS):
        lhs_scr[...] = jnp.concatenate([_pair_lhs(q_ref[:, sl * LANES:(sl + 1) * LANES], lane_row)
                                        for sl in (2 * j, 2 * j + 1)], axis=0)
        m_scr[...] = jnp.full((rows, LANES), NEG_BIG, F32)
        acc_scr[...] = jnp.zeros((rows, LANES), F32)

        def scores(c, s_scr, mx_scr, j=j):
            off = pl.multiple_of(c * tk, tk)
            s = _dot(lhs_scr[...], kd_ref[j, :, pl.ds(off, tk)])
            s_scr[...] = s
            mx_scr[...] = jnp.broadcast_to(jnp.max(s, axis=-1, keepdims=True), (rows, LANES))

        def accumulate(c, s_scr, mx_scr, j=j):
            off = pl.multiple_of(c * tk, tk)
            m_old = m_scr[...]
            m_new = jnp.maximum(m_old, mx_scr[...])
            p = jnp.concatenate([jnp.exp2(s_scr[:, t * LANES:(t + 1) * LANES] - m_new)
                                 for t in range(tk // LANES)], axis=1).astype(BF16)
            pv = _dot(p, va_ref[j, pl.ds(off, tk), :])
            acc_scr[...] = jnp.exp2(m_old - m_new) * acc_scr[...] + pv
            m_scr[...] = m_new

        scores(0, s_a, mx_a)

        def body(i, carry):
            scores(2 * i + 1, s_b, mx_b)
            accumulate(2 * i, s_a, mx_a)
            scores(2 * i + 2, s_a, mx_a)
            accumulate(2 * i + 1, s_b, mx_b)
            return carry

        lax.fori_loop(0, (n_chunks - 1) // 2, body, 0, unroll=2)
        accumulate(n_chunks - 1, s_a, mx_a)
        acc = acc_scr[...]
        o = acc / pltpu.roll(acc, HALF, 1)
        for t, sl in enumerate((2 * j, 2 * j + 1)):
            oa = o[(2 * t) * tq:(2 * t + 1) * tq]
            ob = o[(2 * t + 1) * tq:(2 * t + 2) * tq]
            o_ref[:, sl * LANES:(sl + 1) * LANES] = jnp.where(lane < HALF, oa, pltpu.roll(ob, HALF, 1)).astype(BF16)


def _attn_even(q, kd, va, tq, tk):
    n = q.shape[0]
    n_chunks = kd.shape[2] // tk
    assert n_chunks % 2 == 1
    rows = 4 * tq
    big = lambda w: pltpu.VMEM((rows, w), F32)
    return pl.pallas_call(
        functools.partial(_attn_even_kernel, tq=tq, tk=tk, n_chunks=n_chunks),
        grid=(n // tq,),
        in_specs=[pl.BlockSpec((tq, A_Q), lambda i: (i, 0)), _full(kd.shape), _full(va.shape)],
        out_specs=pl.BlockSpec((tq, A_Q), lambda i: (i, 0)),
        out_shape=jax.ShapeDtypeStruct((n, A_Q), BF16),
        scratch_shapes=[pltpu.VMEM((rows, LANES), BF16), big(tk), big(tk), big(LANES), big(LANES),
                        big(LANES), big(LANES)],
        compiler_params=_cparams(1, 56),
        name="attn_even",
    )(q, kd, va)


def _proj_odd_kernel(x_ref, g_ref, sc_ref, sh_ref, w_ref, bd512_ref, qg_ref, kg_ref,
                     q_ref, k_ref, v_ref, y_ref):
    h = _rms_mod(x_ref[...], g_ref[...], sc_ref[...], sh_ref[...])
    p = _dot(h.astype(BF16), w_ref[...])
    bd = bd512_ref[...]
    q = p[:, :C_W]
    q_ref[...] = (q * lax.rsqrt(_seg_mean_sq(q, bd) + EPS) * qg_ref[...] * (HEAD_DIM ** -0.5)).astype(BF16)
    k = p[:, C_W:2 * C_W]
    k_ref[...] = (k * lax.rsqrt(_seg_mean_sq(k, bd) + EPS) * kg_ref[...]).astype(BF16)
    v_ref[...] = p[:, 2 * C_W:3 * C_W].astype(BF16)
    y_ref[...] = p[:, 3 * C_W:3 * C_W + D_CH] * jax.nn.sigmoid(p[:, 3 * C_W + D_CH:])


def _proj_odd(x, mods, w_in, consts, ts):
    n = x.shape[0]
    g, sc, sh = mods
    bd512, qg, kg = consts
    row = lambda i: (i, 0)
    vec = lambda w: pl.BlockSpec((1, w), lambda i: (0, 0))
    blk = pl.BlockSpec((ts, C_W), row)
    return pl.pallas_call(
        _proj_odd_kernel,
        grid=(n // ts,),
        in_specs=[pl.BlockSpec((ts, D_MODEL), row), vec(D_MODEL), vec(D_MODEL), vec(D_MODEL),
                  _full((D_MODEL, ODD_IN)), _full((C_W, C_W)), vec(C_W), vec(C_W)],
        out_specs=[blk, blk, blk, blk],
        out_shape=[jax.ShapeDtypeStruct((n, C_W), BF16), jax.ShapeDtypeStruct((n, C_W), BF16),
                   jax.ShapeDtypeStruct((n, C_W), BF16), jax.ShapeDtypeStruct((n, D_CH), F32)],
        compiler_params=_cparams(1, 56),
        name="proj_odd",
    )(x, g, sc, sh, w_in, bd512, qg, kg)


NA_BLOCK_ROWS = 8
NA_TOK = NA_BLOCK_ROWS * GRID_W
NA_WIN = NA_ROWS * GRID_W


def _na_kernel(q_ref, kp_ref, kc_ref, kn_ref, vp_ref, vc_ref, vn_ref, kx_ref, vx_ref, bias_ref, o_ref,
               kbuf, vbuf, *, rows):
    i = pl.program_id(0)
    for t, (kr, vr) in enumerate(((kp_ref, vp_ref), (kc_ref, vc_ref), (kn_ref, vn_ref))):
        kbuf[t * NA_TOK:(t + 1) * NA_TOK] = kr[...]
        vbuf[t * NA_TOK:(t + 1) * NA_TOK] = vr[...]
    lane_row = lax.broadcasted_iota(jnp.int32, (1, LANES), 1)
    lane = lax.broadcasted_iota(jnp.int32, (GRID_W, LANES), 1)
    for sl in range(C_W // LANES):
        cols = slice(sl * LANES, (sl + 1) * LANES)
        lhs = _pair_lhs(q_ref[:, cols], lane_row)
        s_ctx = _dot_nt(lhs, kx_ref[:, cols])
        sa, sb, starts = [], [], []
        for j in range(NA_BLOCK_ROWS):
            r = i * NA_BLOCK_ROWS + j
            r0 = jnp.clip(r - NA_ROWS // 2, 0, rows - NA_ROWS)
            start = pl.multiple_of((r0 - (i - 1) * NA_BLOCK_ROWS) * GRID_W, GRID_W)
            starts.append(start)
            qa = lhs[j * GRID_W:(j + 1) * GRID_W]
            qb = lhs[NA_TOK + j * GRID_W:NA_TOK + (j + 1) * GRID_W]
            kw = kbuf[pl.ds(start, NA_WIN), cols]
            s = _dot_nt(jnp.concatenate([qa, qb], axis=0), kw) + bias_ref[r0 - r + NA_ROWS - 1, sl]
            sa.append(s[:GRID_W])
            sb.append(s[GRID_W:])
        s_loc = jnp.concatenate(sa + sb, axis=0)
        m = jnp.maximum(jnp.max(s_loc, axis=-1, keepdims=True), jnp.max(s_ctx, axis=-1, keepdims=True))
        p_loc = jnp.exp(s_loc - m)
        p_ctx = jnp.exp(s_ctx - m)
        l = jnp.sum(p_loc, axis=-1, keepdims=True) + jnp.sum(p_ctx, axis=-1, keepdims=True)
        o_ctx = _dot(p_ctx.astype(BF16), vx_ref[:, cols])
        p_loc = p_loc.astype(BF16)
        for j in range(NA_BLOCK_ROWS):
            ra = slice(j * GRID_W, (j + 1) * GRID_W)
            rb = slice(NA_TOK + j * GRID_W, NA_TOK + (j + 1) * GRID_W)
            vw = vbuf[pl.ds(starts[j], NA_WIN), cols]
            o2 = _dot(jnp.concatenate([p_loc[ra], p_loc[rb]], axis=0), vw)
            oa = (o2[:GRID_W] + o_ctx[ra]) / l[ra]
            ob = (o2[GRID_W:] + o_ctx[rb]) / l[rb]
            o_ref[ra, cols] = jnp.where(lane < HALF, oa, ob).astype(BF16)


def _na_attention(q, k, v, kx, vx, bias):
    n = q.shape[0]
    nb = n // NA_TOK
    blk = lambda f: pl.BlockSpec((NA_TOK, C_W), f)
    prev = lambda i: (jnp.maximum(i - 1, 0), 0)
    cur = lambda i: (i, 0)
    nxt = lambda i: (jnp.minimum(i + 1, nb - 1), 0)
    return pl.pallas_call(
        functools.partial(_na_kernel, rows=n // GRID_W),
        grid=(nb,),
        in_specs=[blk(cur), blk(prev), blk(cur), blk(nxt), blk(prev), blk(cur), blk(nxt),
                  _full(kx.shape), _full(vx.shape), _full(bias.shape)],
        out_specs=blk(cur),
        out_shape=jax.ShapeDtypeStruct((n, C_W), BF16),
        scratch_shapes=[pltpu.VMEM((3 * NA_TOK, C_W), BF16), pltpu.VMEM((3 * NA_TOK, C_W), BF16)],
        compiler_params=_cparams(1, 56),
        name="na_attention",
    )(q, k, k, k, v, v, v, kx, vx, bias)


def _na_bias(rpb):
    cc = np.arange(GRID_W)
    c0 = np.clip(cc - NA_COLS // 2, 0, GRID_W - NA_COLS)
    kc = np.arange(GRID_W)
    inwin = (kc[None, :] >= c0[:, None]) & (kc[None, :] < c0[:, None] + NA_COLS)
    dc = np.clip(kc[None, :] - cc[:, None] + NA_COLS - 1, 0, 2 * NA_COLS - 2)
    n_dr, n_dc = 2 * NA_ROWS - 1, 2 * NA_COLS - 1
    sel = np.zeros((n_dc, GRID_W * GRID_W), np.float32)
    sel[dc.ravel(), np.arange(GRID_W * GRID_W)] = inwin.ravel()
    t2 = jnp.dot(rpb.astype(F32).reshape(C_HEADS * n_dr, n_dc), jnp.asarray(sel), precision=lax.Precision.HIGHEST)
    t2 = t2.reshape(C_HEADS, n_dr, GRID_W, GRID_W) + jnp.asarray(np.where(inwin, 0.0, NEG_BIG), F32)
    t = jnp.stack([t2[:, oi:oi + NA_ROWS] for oi in range(NA_ROWS)], axis=0)
    t = jnp.transpose(t, (0, 1, 3, 2, 4))
    return t.reshape(NA_ROWS, C_HEADS // 2, 2 * GRID_W, NA_WIN)


def _mha_kernel(q_ref, k_ref, v_ref, o_ref):
    n = q_ref.shape[0]
    lane_row = lax.broadcasted_iota(jnp.int32, (1, LANES), 1)
    lane = lax.broadcasted_iota(jnp.int32, (n, LANES), 1)
    for sl in range(C_W // LANES):
        cols = slice(sl * LANES, (sl + 1) * LANES)
        s = _dot_nt(_pair_lhs(q_ref[:, cols], lane_row), k_ref[:, cols])
        p = jnp.exp(s - jnp.max(s, axis=-1, keepdims=True))
        l = jnp.sum(p, axis=-1, keepdims=True)
        o = _dot(p.astype(BF16), v_ref[:, cols]) / l
        o_ref[:, cols] = jnp.where(lane < HALF, o[:n], o[n:]).astype(BF16)


def _mha_small(q, k, v):
    return pl.pallas_call(
        _mha_kernel,
        grid=(1,),
        in_specs=[_full(q.shape), _full(k.shape), _full(v.shape)],
        out_specs=_full(q.shape),
        out_shape=jax.ShapeDtypeStruct(q.shape, BF16),
        compiler_params=_cparams(1, 32),
        name="mha_small",
    )(q, k, v)


def _conv_kernel(y_ref, yp_ref, yn_ref, dw_ref, dwb_ref, lng_ref, lnb_ref, o_ref, ext, *, ts):
    i = pl.program_id(0)
    nb = pl.num_programs(0)
    ext[0:CONV_HALO] = jnp.where(i > 0, yp_ref[...], 0.0)
    ext[CONV_HALO:CONV_HALO + ts] = y_ref[...]
    ext[CONV_HALO + ts:] = jnp.where(i < nb - 1, yn_ref[...], 0.0)
    acc = jnp.zeros((ts, D_CH), F32)
    base = CONV_HALO - CONV_W // 2
    for j in range(CONV_W):
        acc = acc + ext[base + j:base + j + ts, :] * dw_ref[j:j + 1, :]
    yb = acc + dwb_ref[...]
    mu = jnp.mean(yb, axis=-1, keepdims=True)
    var = jnp.mean(jnp.square(yb - mu), axis=-1, keepdims=True)
    z = (yb - mu) * lax.rsqrt(var + EPS) * lng_ref[...] + lnb_ref[...]
    o_ref[...] = (z * jax.nn.sigmoid(z)).astype(BF16)


def _conv_module(y, dw, dwb, lng, lnb, ts):
    n = y.shape[0]
    hb = ts // CONV_HALO
    nh = n // CONV_HALO
    vec = pl.BlockSpec((1, D_CH), lambda i: (0, 0))
    return pl.pallas_call(
        functools.partial(_conv_kernel, ts=ts),
        grid=(n // ts,),
        in_specs=[pl.BlockSpec((ts, D_CH), lambda i: (i, 0)),
                  pl.BlockSpec((CONV_HALO, D_CH), lambda i: (jnp.maximum(i * hb - 1, 0), 0)),
                  pl.BlockSpec((CONV_HALO, D_CH), lambda i: (jnp.minimum((i + 1) * hb, nh - 1), 0)),
                  _full((CONV_W + 1, D_CH)), vec, vec, vec],
        out_specs=pl.BlockSpec((ts, D_CH), lambda i: (i, 0)),
        out_shape=jax.ShapeDtypeStruct((n, D_CH), BF16),
        scratch_shapes=[pltpu.VMEM((ts + 2 * CONV_HALO, D_CH), F32)],
        compiler_params=_cparams(1, 32),
        name="conv_module",
    )(y, y, y, dw, dwb, lng, lnb)


SUBLANES = 8


def _first_max2(vals):
    def first_max(vs):
        m = functools.reduce(jnp.maximum, vs)
        idx = len(vs) - 1
        for k in range(len(vs) - 2, -1, -1):
            idx = jnp.where(vs[k] == m, k, idx)
        return m, idx

    m1, i1 = first_max(vals)
    m2, i2 = first_max([jnp.where(i1 == k, -jnp.inf, v) for k, v in enumerate(vals)])
    return m1, m2, i1, i2


def _pick(idx, vals):
    out = vals[-1]
    for k in range(len(vals) - 2, -1, -1):
        out = jnp.where(idx == k, vals[k], out)
    return out


def _route_t(lt, rbc, tri, carry, ts):
    nk = EXPERTS_PER_GROUP
    aff = jax.nn.sigmoid(lt[:nk * SUBLANES])
    sel = aff + rbc[:nk * SUBLANES]
    sel_k = [sel[k * SUBLANES:(k + 1) * SUBLANES] for k in range(nk)]
    aff_k = [aff[k * SUBLANES:(k + 1) * SUBLANES] for k in range(nk)]
    m1, m2, _, _ = _first_max2(sel_k)
    gscore = m1 + m2
    best, bg = gscore[0:1], jnp.zeros((1, ts), jnp.int32)
    for g in range(1, N_GROUPS):
        upd = gscore[g:g + 1] > best
        best = jnp.where(upd, gscore[g:g + 1], best)
        bg = jnp.where(upd, g, bg)
    sel_b = [_pick(bg, [v[g:g + 1] for g in range(N_GROUPS)]) for v in sel_k]
    aff_b = [_pick(bg, [v[g:g + 1] for g in range(N_GROUPS)]) for v in aff_k]
    _, _, i1, i2 = _first_max2(sel_b)
    a1, a2 = _pick(i1, aff_b), _pick(i2, aff_b)
    e1 = bg * nk + i1
    e2 = bg * nk + i2
    erow = lax.broadcasted_iota(jnp.int32, (N_EXPERTS, ts), 0)
    hit1 = erow == e1
    hit2 = erow == e2
    onehot = jnp.where(hit1 | hit2, 1.0, 0.0)
    before = _dot(onehot.astype(BF16), tri) + carry[:, 0:1]
    r1 = jnp.sum(jnp.where(hit1, before, 0.0), axis=0, keepdims=True)
    r2 = jnp.sum(jnp.where(hit2, before, 0.0), axis=0, keepdims=True)
    new_carry = carry + jnp.sum(onehot, axis=1, keepdims=True)
    tot = a1 + a2
    rows = [e1.astype(F32), e2.astype(F32), a1 / tot, a2 / tot, r1, r2, jnp.zeros((2, ts), F32)]
    return jnp.concatenate(rows, axis=0), new_carry


def _post_mix_kernel(x_ref, a_ref, b_ref, w_ref, g1_ref, ng_ref, sc_ref, sh_ref, rwh_ref, rwl_ref, rb_ref, tri_ref,
                     xo_ref, h2_ref, route_ref, cnt_ref, *, ts):
    @pl.when(pl.program_id(0) == 0)
    def _():
        cnt_ref[...] = jnp.zeros_like(cnt_ref)

    half = w_ref.shape[0] // 2
    mo = _dot(a_ref[...], w_ref[:half]) + _dot(b_ref[...], w_ref[half:])
    xn = x_ref[...] + g1_ref[...] * mo
    xo_ref[...] = xn
    h2 = _rms_mod(xn, ng_ref[...], sc_ref[...], sh_ref[...])
    h2_ref[...] = h2
    hh, hl = _split_bf16(h2)
    lt = _dot_nt(rwh_ref[...], hh) + _dot_nt(rwh_ref[...], hl) + _dot_nt(rwl_ref[...], hh)
    rt, cnt = _route_t(lt, rb_ref[...], tri_ref[...], cnt_ref[...], ts)
    cnt_ref[...] = cnt
    route_ref[...] = jnp.concatenate([rt, jnp.zeros((LANES - SUBLANES, ts), F32)], axis=0).T


def _post_mix(x, a, b, w_out, mods, router, ts):
    n = x.shape[0]
    g1, ng, sc, sh = mods
    rwh, rwl, rbc = router
    tri = jnp.asarray(np.triu(np.ones((ts, ts), np.float32), 1), dtype=BF16)
    row = lambda i: (i, 0)
    vec = lambda w: pl.BlockSpec((1, w), lambda i: (0, 0))
    return pl.pallas_call(
        functools.partial(_post_mix_kernel, ts=ts),
        grid=(n // ts,),
        in_specs=[pl.BlockSpec((ts, D_MODEL), row), pl.BlockSpec((ts, a.shape[1]), row),
                  pl.BlockSpec((ts, b.shape[1]), row), _full(w_out.shape),
                  vec(D_MODEL), vec(D_MODEL), vec(D_MODEL), vec(D_MODEL),
                  _full(rwh.shape), _full(rwl.shape), _full(rbc.shape), _full(tri.shape)],
        out_specs=[pl.BlockSpec((ts, D_MODEL), row), pl.BlockSpec((ts, D_MODEL), row),
                   pl.BlockSpec((ts, LANES), row), _full((N_EXPERTS, LANES))],
        out_shape=[jax.ShapeDtypeStruct((n, D_MODEL), F32), jax.ShapeDtypeStruct((n, D_MODEL), F32),
                   jax.ShapeDtypeStruct((n, LANES), F32), jax.ShapeDtypeStruct((N_EXPERTS, LANES), F32)],
        compiler_params=_cparams(1, 56),
        name="post_mix",
    )(x, a, b, w_out, g1, ng, sc, sh, rwh, rwl, rbc, tri)


def _dispatch_kernel(dest_ref, h_ref, buf_in_ref, buf_ref, sem, *, ts):
    del buf_in_ref

    def row_copy(r, k):
        return pltpu.make_async_copy(h_ref.at[pl.ds(r, 1)], buf_ref.at[pl.ds(dest_ref[0, 0, 2 * r + k], 1)], sem)

    def issue(r, carry):
        row_copy(r, 0).start()
        row_copy(r, 1).start()
        return carry

    def drain(r, carry):
        row_copy(r, 0).wait()
        row_copy(r, 1).wait()
        return carry

    lax.fori_loop(0, ts, issue, 0, unroll=DMA_UNROLL)
    lax.fori_loop(0, ts, drain, 0, unroll=DMA_UNROLL)


def _dispatch(h2, dest3, n_rows, ts):
    n = h2.shape[0]
    buf0 = jnp.zeros((n_rows, D_MODEL), F32)
    return pl.pallas_call(
        functools.partial(_dispatch_kernel, ts=ts),
        grid=(n // ts,),
        in_specs=[pl.BlockSpec((1, 1, 2 * ts), lambda i: (i, 0, 0), memory_space=pltpu.SMEM),
                  pl.BlockSpec((ts, D_MODEL), lambda i: (i, 0)),
                  pl.BlockSpec(memory_space=pl.ANY)],
        out_specs=pl.BlockSpec(memory_space=pl.ANY),
        out_shape=jax.ShapeDtypeStruct((n_rows, D_MODEL), F32),
        scratch_shapes=[pltpu.SemaphoreType.DMA(())],
        input_output_aliases={2: 0},
        compiler_params=_cparams(1, 32),
        name="moe_dispatch",
    )(dest3, h2, buf0)


def _expert_kernel(blk_e_ref, n_used_ref, x_ref, wg_ref, wu_ref, wd_ref, y_ref, wg_s, wu_s, wd_s):
    i = pl.program_id(0)

    @pl.when((i == 0) | (blk_e_ref[i] != blk_e_ref[jnp.maximum(i - 1, 0)]))
    def _():
        wg_s[...] = wg_ref[0, 0].astype(BF16)
        wu_s[...] = wu_ref[0, 0].astype(BF16)
        wd_s[...] = wd_ref[0, 0].astype(BF16)

    @pl.when(i < n_used_ref[0])
    def _():
        xb = x_ref[...].astype(BF16)
        gate = _dot(xb, wg_s[...])
        up = _dot(xb, wu_s[...])
        hmid = gate * jax.nn.sigmoid(gate) * up
        y_ref[...] = _dot(hmid.astype(BF16), wd_s[...])

    @pl.when(i >= n_used_ref[0])
    def _():
        y_ref[...] = jnp.zeros_like(y_ref)


def _experts(buf, blk_e, n_used, layer, wg, wu, wd):
    n_rows = buf.shape[0]
    nb = n_rows // MOE_BLOCK
    grid_spec = pltpu.PrefetchScalarGridSpec(
        num_scalar_prefetch=2,
        grid=(nb,),
        in_specs=[pl.BlockSpec((MOE_BLOCK, D_MODEL), lambda i, be, nu: (i, 0)),
                  pl.BlockSpec((1, 1, D_MODEL, D_EXPERT), lambda i, be, nu: (layer, be[i], 0, 0)),
                  pl.BlockSpec((1, 1, D_MODEL, D_EXPERT), lambda i, be, nu: (layer, be[i], 0, 0)),
                  pl.BlockSpec((1, 1, D_EXPERT, D_MODEL), lambda i, be, nu: (layer, be[i], 0, 0))],
        out_specs=pl.BlockSpec((MOE_BLOCK, D_MODEL), lambda i, be, nu: (i, 0)),
        scratch_shapes=[pltpu.VMEM((D_MODEL, D_EXPERT), BF16), pltpu.VMEM((D_MODEL, D_EXPERT), BF16),
                        pltpu.VMEM((D_EXPERT, D_MODEL), BF16)],
    )
    return pl.pallas_call(
        _expert_kernel,
        grid_spec=grid_spec,
        out_shape=jax.ShapeDtypeStruct((n_rows, D_MODEL), F32),
        compiler_params=_cparams(1, 48),
        name="moe_experts",
    )(blk_e, n_used, buf, wg, wu, wd)


def _combine_kernel(dest_ref, x_ref, route_ref, g2_ref, y_ref, o_ref, g0, g1, sem, *, ts):
    def row_copy(r, k):
        dst = g0 if k == 0 else g1
        return pltpu.make_async_copy(y_ref.at[pl.ds(dest_ref[0, 0, 2 * r + k], 1)], dst.at[pl.ds(r, 1)], sem)

    def issue(r, carry):
        row_copy(r, 0).start()
        row_copy(r, 1).start()
        return carry

    def drain(r, carry):
        row_copy(r, 0).wait()
        row_copy(r, 1).wait()
        return carry

    lax.fori_loop(0, ts, issue, 0, unroll=DMA_UNROLL)
    lax.fori_loop(0, ts, drain, 0, unroll=DMA_UNROLL)
    route = route_ref[...]
    f = g0[...] * route[:, 2:3] + g1[...] * route[:, 3:4]
    o_ref[...] = x_ref[...] + g2_ref[...] * f


def _combine(x, route, g2, y, dest3, ts):
    n = x.shape[0]
    row = lambda i: (i, 0)
    return pl.pallas_call(
        functools.partial(_combine_kernel, ts=ts),
        grid=(n // ts,),
        in_specs=[pl.BlockSpec((1, 1, 2 * ts), lambda i: (i, 0, 0), memory_space=pltpu.SMEM),
                  pl.BlockSpec((ts, D_MODEL), row), pl.BlockSpec((ts, LANES), row),
                  pl.BlockSpec((1, D_MODEL), lambda i: (0, 0)),
                  pl.BlockSpec(memory_space=pl.ANY)],
        out_specs=pl.BlockSpec((ts, D_MODEL), row),
        out_shape=jax.ShapeDtypeStruct((n, D_MODEL), F32),
        scratch_shapes=[pltpu.VMEM((ts, D_MODEL), F32), pltpu.VMEM((ts, D_MODEL), F32),
                        pltpu.SemaphoreType.DMA(())],
        compiler_params=_cparams(1, 48),
        name="moe_combine",
    )(dest3, x, route, g2, y)


def _moe(x, h2, route, counts, g2, layer, wg, wu, wd, ts):
    n = x.shape[0]
    experts = jnp.arange(N_EXPERTS, dtype=jnp.int32)
    cnt = counts[:, 0].astype(jnp.int32)
    pcounts = (cnt + MOE_BLOCK - 1) // MOE_BLOCK * MOE_BLOCK
    pend = jnp.cumsum(pcounts)
    pstart = pend - pcounts
    e = route[:, 0:2].astype(jnp.int32)
    rank = route[:, 4:6].astype(jnp.int32)
    dest = jnp.sum(jnp.where(e[:, :, None] == experts, pstart, 0), axis=-1) + rank
    n_blocks = -(-2 * n // MOE_BLOCK) + N_EXPERTS
    blk_pos = jnp.arange(n_blocks, dtype=jnp.int32) * MOE_BLOCK
    blk_e = jnp.minimum(jnp.sum((pend[None, :] <= blk_pos[:, None]).astype(jnp.int32), axis=1), N_EXPERTS - 1)
    n_used = (pend[-1:] // MOE_BLOCK).astype(jnp.int32)
    dest3 = dest.reshape(n // ts, 1, 2 * ts)
    buf = _dispatch(h2, dest3, n_blocks * MOE_BLOCK, ts)
    y = _experts(buf, blk_e, n_used, layer, wg, wu, wd)
    return _combine(x, route, g2, y, dest3, ts)


def _block_diag_ones(width):
    idx = np.arange(width) // HALF
    return jnp.asarray((idx[:, None] == idx[None, :]).astype(np.float32), dtype=BF16)


def _rope_tables(n_tok):
    t = jnp.arange(n_tok, dtype=jnp.int32)
    pos = jnp.stack([t // GRID_W, t % GRID_W], axis=-1).astype(F32)
    inv = ROPE_THETA ** (-jnp.arange(ROPE_FREQS, dtype=F32) / ROPE_FREQS)
    ang = pos[:, :, None] * inv
    cos, sin = jnp.cos(ang), jnp.sin(ang)
    c = jnp.concatenate([cos, cos], axis=-1).reshape(n_tok, HEAD_DIM)
    s = jnp.concatenate([-sin, sin], axis=-1).reshape(n_tok, HEAD_DIM)
    return jnp.tile(c, (1, 2)), jnp.tile(s, (1, 2))


def _tile_vec(v, reps):
    return jnp.tile(v.astype(F32), reps).reshape(1, -1)


def kernel(x, c, ctx, c_ctx, router_w, router_b, ada_w, ada_b, norm1_g, norm2_g, ev_w_in, ev_w_out, a_q_norm,
           a_k_norm, b_v_norm, b_ws, b_bs, od_w_in, od_w_out, c_q_norm, c_k_norm, c_rpb, d_dw, d_dw_b, d_ln_g,
           d_ln_b, moe_w_gate, moe_w_up, moe_w_down):
    assert x.shape[0] == 1 and ctx.shape[0] == 1
    s_len, n_ctx = x.shape[1], ctx.shape[1]
    ts_lat = min(512, s_len)
    ts_ctx = n_ctx
    x_lat = x[0]
    x_ctx = ctx[0]

    cc = jnp.zeros((8, D_MODEL), F32).at[0].set(c[0]).at[1].set(c_ctx)
    mods = _modulation(cc, ada_w, ada_b)

    def mod(l, row, k):
        return mods[l, row, k * D_MODEL:(k + 1) * D_MODEL].reshape(1, D_MODEL)

    bd512 = _block_diag_ones(A_Q)
    bd128 = _block_diag_ones(LANES)
    cos_lat, sin_lat = _rope_tables(s_len)
    cos_ctx, sin_ctx = jnp.ones((n_ctx, LANES), F32), jnp.zeros((n_ctx, LANES), F32)
    e_of_row = np.arange(N_EXPERTS)
    row_of_e = (e_of_row % EXPERTS_PER_GROUP) * SUBLANES + e_of_row // EXPERTS_PER_GROUP
    rw = jnp.zeros((LANES, D_MODEL), F32).at[row_of_e].set(router_w.T)
    rwh = rw.astype(BF16)
    rwl = (rw - rwh.astype(F32)).astype(BF16)
    rbc = jnp.zeros((LANES, 1), F32).at[row_of_e, 0].set(router_b)
    router = (rwh, rwl, rbc)

    for l in range(DEPTH):
        need_ctx = l < DEPTH - 1
        i = l // 2
        g_n1 = norm1_g[l].reshape(1, D_MODEL)
        g_n2 = norm2_g[l].reshape(1, D_MODEL)
        m_lat = [mod(l, 0, k) for k in range(6)]
        m_ctx = [mod(l, 1, k) for k in range(6)]
        if l % 2 == 0:
            w_in = ev_w_in[i].astype(BF16)
            w_out = ev_w_out[i].astype(BF16)
            consts = (bd512, bd128, _tile_vec(a_q_norm[i], A_HEADS), _tile_vec(a_k_norm[i], A_KV_HEADS),
                      b_v_norm[i].reshape(1, B_WIDTH).astype(F32), b_ws[i].astype(BF16),
                      jnp.repeat(b_bs[i].T, HALF, axis=1))
            q_l, kd_l, v_l, b_l = _proj_even(x_lat, (g_n1, m_lat[1], m_lat[0]), w_in, consts, cos_lat, sin_lat, ts_lat)
            q_c, kd_c, v_c, b_c = _proj_even(x_ctx, (g_n1, m_ctx[1], m_ctx[0]), w_in, consts, cos_ctx, sin_ctx, ts_ctx)
            kd_cc = jnp.concatenate([kd_c, kd_c], axis=2)
            v_cc = jnp.concatenate([v_c, jnp.zeros_like(v_c)], axis=1)
            kd_all = jnp.concatenate([kd_l, kd_cc], axis=2)
            v_all = jnp.concatenate([v_l, v_cc], axis=1)
            mix_l = (_attn_even(q_l, kd_all, v_all, min(ATTN_TQ, s_len), ATTN_TK), b_l)
            if need_ctx:
                mix_c = (_attn_even(q_c, kd_cc, v_cc, ATTN_TQ, ATTN_TK), b_c)
        else:
            w_in = od_w_in[i].astype(BF16)
            w_out = od_w_out[i].astype(BF16)
            consts = (bd512, _tile_vec(c_q_norm[i], C_HEADS), _tile_vec(c_k_norm[i], C_HEADS))
            q_l, k_l, v_l, y_l = _proj_odd(x_lat, (g_n1, m_lat[1], m_lat[0]), w_in, consts, ts_lat)
            q_c, k_c, v_c, y_c = _proj_odd(x_ctx, (g_n1, m_ctx[1], m_ctx[0]), w_in, consts, ts_ctx)
            dw = jnp.zeros((CONV_W + 1, D_CH), F32).at[:CONV_W].set(d_dw[i])
            conv_p = (dw, d_dw_b[i].reshape(1, D_CH), d_ln_g[i].reshape(1, D_CH), d_ln_b[i].reshape(1, D_CH))
            mix_l = (_na_attention(q_l, k_l, v_l, k_c, v_c, _na_bias(c_rpb[i])), _conv_module(y_l, *conv_p, ts_lat))
            if need_ctx:
                mix_c = (_mha_small(q_c, k_c, v_c), _conv_module(y_c, *conv_p, ts_ctx))
        moe_w = (l, moe_w_gate, moe_w_up, moe_w_down)
        xn, h2, route, cnt = _post_mix(x_lat, *mix_l, w_out, (m_lat[2], g_n2, m_lat[4], m_lat[3]), router, ts_lat)
        x_lat = _moe(xn, h2, route, cnt, m_lat[5], *moe_w, ts_lat)
        if need_ctx:
            xn, h2, route, cnt = _post_mix(x_ctx, *mix_c, w_out, (m_ctx[2], g_n2, m_ctx[4], m_ctx[3]), router, ts_ctx)
            x_ctx = _moe(xn, h2, route, cnt, m_ctx[5], *moe_w, ts_ctx)
    return x_lat[None]
```

```python
import functools

import numpy as np
import jax
import jax.numpy as jnp
from jax import lax
from jax.experimental import pallas as pl
from jax.experimental.pallas import tpu as pltpu

F32 = jnp.float32
BF16 = jnp.bfloat16

D_MODEL = 1024
DEPTH = 4
GRID_W = 64
HEAD_DIM = 64
EPS = 1e-6
A_HEADS = 8
A_KV_HEADS = 2
A_Q = A_HEADS * HEAD_DIM
A_KV = A_KV_HEADS * HEAD_DIM
A_IN = A_Q + 2 * A_KV
ROPE_THETA = 10000.0
ROPE_FREQS = HEAD_DIM // 4
B_GROUPS = 8
B_WIDTH = 512
CHUNK = 128
C_HEADS = 8
C_W = C_HEADS * HEAD_DIM
NA_ROWS = 8
NA_COLS = 16
D_CH = 512
CONV_W = 31
EVEN_IN = A_IN + 2 * B_WIDTH
ODD_IN = 3 * C_W + 2 * D_CH
N_EXPERTS = 16
N_GROUPS = 4
EXPERTS_PER_GROUP = 4
D_EXPERT = 512
MOE_BLOCK = 256

LANES = 128
SUBLANES = 8
HALF = 64
NEG_BIG = -1e30
CONV_HALO = 16
DMA_UNROLL = 8
ATTN_TQ = 256
ATTN_TK = 512
QK_SCALE_LOG2 =HEAD_DIM ** -0.5 * 1.4426950408889634


def _cparams(n_axes, vmem_mb):
    return pltpu.CompilerParams(dimension_semantics=("arbitrary",) * n_axes,
                                vmem_limit_bytes=vmem_mb << 20)


def _full(shape):
    nd = len(shape)
    return pl.BlockSpec(shape, lambda *_: (0,) * nd)


def _dot(a, b):
    return jnp.dot(a, b, preferred_element_type=F32)


def _dot_nt(a, b):
    return lax.dot_general(a, b, (((1,), (1,)), ((), ())), preferred_element_type=F32)


def _split_bf16(x):
    hi = x.astype(BF16)
    lo = (x - hi.astype(F32)).astype(BF16)
    return hi, lo


def _seg_mean_sq(x, bd):
    hi, lo = _split_bf16(x * x)
    return (_dot(hi, bd) + _dot(lo, bd)) * (1.0 / HALF)


def _rms_mod(x, g, sc, sh):
    ms = jnp.mean(x * x, axis=-1, keepdims=True)
    return x * lax.rsqrt(ms + EPS) * g * (1.0 + sc) + sh


def _rope(x, c, sg, lane):
    sw = jnp.where((lane % 32) < 16, pltpu.roll(x, LANES - 16, 1), pltpu.roll(x, 16, 1))
    return x * c + sw * sg


def _pair_lhs(qs, lane_row):
    ma = jnp.where(lane_row < HALF, 1.0, 0.0).astype(BF16)
    mb = jnp.where(lane_row < HALF, 0.0, 1.0).astype(BF16)
    return jnp.concatenate([qs * ma, qs * mb], axis=0)


def _store_row_tiles(ref, val, t):
    for k in range(SUBLANES):
        ref[pl.ds(k, t, stride=SUBLANES), :] = val[:, k * LANES:(k + 1) * LANES]


def _load_row_tiles(ref, t):
    return jnp.concatenate([ref[pl.ds(k, t, stride=SUBLANES), :] for k in range(SUBLANES)], axis=1)


def _mod_kernel(cc_ref, w_ref, b_ref, o_ref):
    cc = cc_ref[...]
    s = cc * jax.nn.sigmoid(cc)
    sh, sl = _split_bf16(s)
    wh, wl = _split_bf16(w_ref[0])
    o_ref[0] = _dot(sh, wh) + _dot(sl, wh) + _dot(sh, wl) + b_ref[0]


def _modulation(cc, ada_w, ada_b):
    tn = 1024
    return pl.pallas_call(
        _mod_kernel,
        grid=(DEPTH, 6 * D_MODEL // tn),
        in_specs=[pl.BlockSpec((8, D_MODEL), lambda l, j: (0, 0)),
                  pl.BlockSpec((1, D_MODEL, tn), lambda l, j: (l, 0, j)),
                  pl.BlockSpec((1, 1, tn), lambda l, j: (l, 0, j))],
        out_specs=pl.BlockSpec((1, 8, tn), lambda l, j: (l, 0, j)),
        out_shape=jax.ShapeDtypeStruct((DEPTH, 8, 6 * D_MODEL), F32),
        compiler_params=_cparams(2, 32),
        name="modulation",
    )(cc, ada_w, ada_b.reshape(DEPTH, 1, 6 * D_MODEL))


def _proj_even_kernel(x_ref, g_ref, sc_ref, sh_ref, w_ref, bd512_ref, bd128_ref, qg_ref, kg_ref, vg_ref,
                      cos_ref, sin_ref, ws_ref, bsb_ref, q_ref, kd_ref, v_ref, b_ref, *, ts):
    h = _rms_mod(x_ref[...], g_ref[...], sc_ref[...], sh_ref[...])
    p = _dot(h.astype(BF16), w_ref[...])
    lane = lax.broadcasted_iota(jnp.int32, (ts, LANES), 1)
    c = cos_ref[...]
    sg = sin_ref[...]

    q = p[:, :A_Q]
    qn = q * lax.rsqrt(_seg_mean_sq(q, bd512_ref[...]) + EPS) * qg_ref[...]
    for sl in range(A_Q // LANES):
        xs = qn[:, sl * LANES:(sl + 1) * LANES]
        q_ref[:, sl * LANES:(sl + 1) * LANES] = (_rope(xs, c, sg, lane) * QK_SCALE_LOG2).astype(BF16)

    k = p[:, A_Q:A_Q + A_KV]
    kn = k * lax.rsqrt(_seg_mean_sq(k, bd128_ref[...]) + EPS) * kg_ref[...]
    kt = _rope(kn, c, sg, lane).T
    for j in range(A_KV_HEADS):
        kj = kt[j * HALF:(j + 1) * HALF]
        kd_ref[j] = jnp.concatenate([kj, kj], axis=0).astype(BF16)
    v = p[:, A_Q + A_KV:A_IN]
    v_ref[0] = jnp.where(lane < HALF, v, 1.0).astype(BF16)
    v_ref[1] = jnp.where(lane < HALF, pltpu.roll(v, HALF, 1), 1.0).astype(BF16)

    zu = jax.nn.gelu(p[:, A_IN:A_IN + B_WIDTH])
    zv = jax.nn.gelu(p[:, A_IN + B_WIDTH:])
    vn = (zv * lax.rsqrt(_seg_mean_sq(zv, bd512_ref[...]) + EPS) * vg_ref[...]).astype(BF16)
    lane_c = lax.broadcasted_iota(jnp.int32, (CHUNK, LANES), 1)
    for ch in range(ts // CHUNK):
        rows = slice(ch * CHUNK, (ch + 1) * CHUNK)
        for sl in range(B_WIDTH // LANES):
            cols = slice(sl * LANES, (sl + 1) * LANES)
            vs = vn[rows, cols]
            mixed = jnp.where(lane_c < HALF, _dot(ws_ref[2 * sl], vs), _dot(ws_ref[2 * sl + 1], vs))
            b_ref[rows, cols] = (zu[rows, cols] * (mixed + bsb_ref[:, cols])).astype(BF16)


def _proj_even(x, mods, w_in, consts, cos, sin, ts):
    n = x.shape[0]
    g, sc, sh = mods
    bd512, bd128, qg, kg, vg, ws, bsb = consts
    row = lambda i: (i, 0)
    vec = lambda w: pl.BlockSpec((1, w), lambda i: (0, 0))
    return pl.pallas_call(
        functools.partial(_proj_even_kernel, ts=ts),
        grid=(n // ts,),
        in_specs=[pl.BlockSpec((ts, D_MODEL), row), vec(D_MODEL), vec(D_MODEL), vec(D_MODEL),
                  _full((D_MODEL, EVEN_IN)), _full((A_Q, A_Q)), _full((LANES, LANES)),
                  vec(A_Q), vec(A_KV), vec(B_WIDTH),
                  pl.BlockSpec((ts, LANES), row), pl.BlockSpec((ts, LANES), row),
                  _full((B_GROUPS, CHUNK, CHUNK)), _full((CHUNK, B_WIDTH))],
        out_specs=[pl.BlockSpec((ts, A_Q), row),
                   pl.BlockSpec((A_KV_HEADS, LANES, ts), lambda i: (0, 0, i)),
                   pl.BlockSpec((A_KV_HEADS, ts, LANES), lambda i: (0, i, 0)),
                   pl.BlockSpec((ts, B_WIDTH), row)],
        out_shape=[jax.ShapeDtypeStruct((n, A_Q), BF16),
                   jax.ShapeDtypeStruct((A_KV_HEADS, LANES, n), BF16),
                   jax.ShapeDtypeStruct((A_KV_HEADS, n, LANES), BF16),
                   jax.ShapeDtypeStruct((n, B_WIDTH), BF16)],
        compiler_params=_cparams(1, 56),
        name="proj_even",
    )(x, g, sc, sh, w_in, bd512, bd128, qg, kg, vg, cos, sin, ws, bsb)


def _attn_even_kernel(q_ref, kd_ref, va_ref, o_ref, lhs_scr, s_a, s_b, mx_a, mx_b, m_scr, acc_scr,
                      *, tq, tk, n_chunks):
    rows = 4 * tq
    lane_row = lax.broadcasted_iota(jnp.int32, (1, LANES), 1)
    lane = lax.broadcasted_iota(jnp.int32, (tq, LANES), 1)
    for j in range(A_KV_HEADS):
        lhs_scr[...] = jnp.concatenate([_pair_lhs(q_ref[:, sl * LANES:(sl + 1) * LANES], lane_row)
                                        for sl in (2 * j, 2 * j + 1)], axis=0)
        m_scr[...] = jnp.full((rows, LANES), NEG_BIG, F32)
        acc_scr[...] = jnp.zeros((rows, LANES), F32)

        def scores(c, s_scr, mx_scr, j=j):
            off = pl.multiple_of(c * tk, tk)
            s = _dot(lhs_scr[...], kd_ref[j, :, pl.ds(off, tk)])
            s_scr[...] = s
            mx_scr[...] = jnp.broadcast_to(jnp.max(s, axis=-1, keepdims=True), (rows, LANES))

        def accumulate(c, s_scr, mx_scr, j=j):
            off = pl.multiple_of(c * tk, tk)
            m_old = m_scr[...]
            m_new = jnp.maximum(m_old, mx_scr[...])
            p = jnp.concatenate([jnp.exp2(s_scr[:, t * LANES:(t + 1) * LANES] - m_new)
                                 for t in range(tk // LANES)], axis=1).astype(BF16)
            pv = _dot(p, va_ref[j, pl.ds(off, tk), :])
            acc_scr[...] = jnp.exp2(m_old - m_new) * acc_scr[...] + pv
            m_scr[...] = m_new

        scores(0, s_a, mx_a)

        def body(i, carry):
            scores(2 * i + 1, s_b, mx_b)
            accumulate(2 * i, s_a, mx_a)
            scores(2 * i + 2, s_a, mx_a)
            accumulate(2 * i + 1, s_b, mx_b)
            return carry

        lax.fori_loop(0, (n_chunks - 1) // 2, body, 0, unroll=2)
        accumulate(n_chunks - 1, s_a, mx_a)
        acc = acc_scr[...]
        o = acc / pltpu.roll(acc, HALF, 1)
        for t, sl in enumerate((2 * j, 2 * j + 1)):
            oa = o[(2 * t) * tq:(2 * t + 1) * tq]
            ob = o[(2 * t + 1) * tq:(2 * t + 2) * tq]
            o_ref[:, sl * LANES:(sl + 1) * LANES] = jnp.where(lane < HALF, oa, pltpu.roll(ob, HALF, 1)).astype(BF16)


def _attn_even(q, kd, va, tq, tk):
    n = q.shape[0]
    n_chunks = kd.shape[2] // tk
    assert n_chunks % 2 == 1
    rows = 4 * tq
    big = lambda w: pltpu.VMEM((rows, w), F32)
    return pl.pallas_call(
        functools.partial(_attn_even_kernel, tq=tq, tk=tk, n_chunks=n_chunks),
        grid=(n // tq,),
        in_specs=[pl.BlockSpec((tq, A_Q), lambda i: (i, 0)), _full(kd.shape), _full(va.shape)],
        out_specs=pl.BlockSpec((tq, A_Q), lambda i: (i, 0)),
        out_shape=jax.ShapeDtypeStruct((n, A_Q), BF16),
        scratch_shapes=[pltpu.VMEM((rows, LANES), BF16), big(tk), big(tk), big(LANES), big(LANES),
                        big(LANES), big(LANES)],
        compiler_params=_cparams(1, 56),
        name="attn_even",
    )(q, kd, va)


def _proj_odd_kernel(x_ref, g_ref, sc_ref, sh_ref, w_ref, bd512_ref, qg_ref, kg_ref,
                     q_ref, k_ref, v_ref, y_ref):
    h = _rms_mod(x_ref[...], g_ref[...], sc_ref[...], sh_ref[...])
    p = _dot(h.astype(BF16), w_ref[...])
    bd = bd512_ref[...]
    q = p[:, :C_W]
    q_ref[...] = (q * lax.rsqrt(_seg_mean_sq(q, bd) + EPS) * qg_ref[...] * (HEAD_DIM ** -0.5)).astype(BF16)
    k = p[:, C_W:2 * C_W]
    k_ref[...] = (k * lax.rsqrt(_seg_mean_sq(k, bd) + EPS) * kg_ref[...]).astype(BF16)
    v_ref[...] = p[:, 2 * C_W:3 * C_W].astype(BF16)
    y_ref[...] = p[:, 3 * C_W:3 * C_W + D_CH] * jax.nn.sigmoid(p[:, 3 * C_W + D_CH:])


def _proj_odd(x, mods, w_in, consts, ts):
    n = x.shape[0]
    g, sc, sh = mods
    bd512, qg, kg = consts
    row = lambda i: (i, 0)
    vec = lambda w: pl.BlockSpec((1, w), lambda i: (0, 0))
    blk = pl.BlockSpec((ts, C_W), row)
    return pl.pallas_call(
        _proj_odd_kernel,
        grid=(n // ts,),
        in_specs=[pl.BlockSpec((ts, D_MODEL), row), vec(D_MODEL), vec(D_MODEL), vec(D_MODEL),
                  _full((D_MODEL, ODD_IN)), _full((C_W, C_W)), vec(C_W), vec(C_W)],
        out_specs=[blk, blk, blk, blk],
        out_shape=[jax.ShapeDtypeStruct((n, C_W), BF16), jax.ShapeDtypeStruct((n, C_W), BF16),
                   jax.ShapeDtypeStruct((n, C_W), BF16), jax.ShapeDtypeStruct((n, D_CH), F32)],
        compiler_params=_cparams(1, 56),
        name="proj_odd",
    )(x, g, sc, sh, w_in, bd512, qg, kg)


NA_BLOCK_ROWS = 8
NA_TOK = NA_BLOCK_ROWS * GRID_W
NA_WIN = NA_ROWS * GRID_W


def _na_kernel(q_ref, kp_ref, kc_ref, kn_ref, vp_ref, vc_ref, vn_ref, kx_ref, vx_ref, bias_ref, o_ref,
               kbuf, vbuf, *, rows):
    i = pl.program_id(0)
    for t, (kr, vr) in enumerate(((kp_ref, vp_ref), (kc_ref, vc_ref), (kn_ref, vn_ref))):
        kbuf[t * NA_TOK:(t + 1) * NA_TOK] = kr[...]
        vbuf[t * NA_TOK:(t + 1) * NA_TOK] = vr[...]
    lane_row = lax.broadcasted_iota(jnp.int32, (1, LANES), 1)
    lane = lax.broadcasted_iota(jnp.int32, (GRID_W, LANES), 1)
    for sl in range(C_W // LANES):
        cols = slice(sl * LANES, (sl + 1) * LANES)
        lhs = _pair_lhs(q_ref[:, cols], lane_row)
        s_ctx = _dot_nt(lhs, kx_ref[:, cols])
        sa, sb, starts = [], [], []
        for j in range(NA_BLOCK_ROWS):
            r = i * NA_BLOCK_ROWS + j
            r0 = jnp.clip(r - NA_ROWS // 2, 0, rows - NA_ROWS)
            start = pl.multiple_of((r0 - (i - 1) * NA_BLOCK_ROWS) * GRID_W, GRID_W)
            starts.append(start)
            qa = lhs[j * GRID_W:(j + 1) * GRID_W]
            qb = lhs[NA_TOK + j * GRID_W:NA_TOK + (j + 1) * GRID_W]
            kw = kbuf[pl.ds(start, NA_WIN), cols]
            s = _dot_nt(jnp.concatenate([qa, qb], axis=0), kw) + bias_ref[r0 - r + NA_ROWS - 1, sl]
            sa.append(s[:GRID_W])
            sb.append(s[GRID_W:])
        s_loc = jnp.concatenate(sa + sb, axis=0)
        m = jnp.maximum(jnp.max(s_loc, axis=-1, keepdims=True), jnp.max(s_ctx, axis=-1, keepdims=True))
        p_loc = jnp.exp(s_loc - m)
        p_ctx = jnp.exp(s_ctx - m)
        l = jnp.sum(p_loc, axis=-1, keepdims=True) + jnp.sum(p_ctx, axis=-1, keepdims=True)
        o_ctx = _dot(p_ctx.astype(BF16), vx_ref[:, cols])
        p_loc = p_loc.astype(BF16)
        for j in range(NA_BLOCK_ROWS):
            ra = slice(j * GRID_W, (j + 1) * GRID_W)
            rb = slice(NA_TOK + j * GRID_W, NA_TOK + (j + 1) * GRID_W)
            vw = vbuf[pl.ds(starts[j], NA_WIN), cols]
            o2 = _dot(jnp.concatenate([p_loc[ra], p_loc[rb]], axis=0), vw)
            oa = (o2[:GRID_W] + o_ctx[ra]) / l[ra]
            ob = (o2[GRID_W:] + o_ctx[rb]) / l[rb]
            o_ref[ra, cols] = jnp.where(lane < HALF, oa, ob).astype(BF16)


def _na_attention(q, k, v, kx, vx, bias):
    n = q.shape[0]
    nb = n // NA_TOK
    blk = lambda f: pl.BlockSpec((NA_TOK, C_W), f)
    prev = lambda i: (jnp.maximum(i - 1, 0), 0)
    cur = lambda i: (i, 0)
    nxt = lambda i: (jnp.minimum(i + 1, nb - 1), 0)
    return pl.pallas_call(
        functools.partial(_na_kernel, rows=n // GRID_W),
        grid=(nb,),
        in_specs=[blk(cur), blk(prev), blk(cur), blk(nxt), blk(prev), blk(cur), blk(nxt),
                  _full(kx.shape), _full(vx.shape), _full(bias.shape)],
        out_specs=blk(cur),
        out_shape=jax.ShapeDtypeStruct((n, C_W), BF16),
        scratch_shapes=[pltpu.VMEM((3 * NA_TOK, C_W), BF16), pltpu.VMEM((3 * NA_TOK, C_W), BF16)],
        compiler_params=_cparams(1, 56),
        name="na_attention",
    )(q, k, k, k, v, v, v, kx, vx, bias)


def _na_bias(rpb):
    cc = np.arange(GRID_W)
    c0 = np.clip(cc - NA_COLS // 2, 0, GRID_W - NA_COLS)
    kc = np.arange(GRID_W)
    inwin = (kc[None, :] >= c0[:, None]) & (kc[None, :] < c0[:, None] + NA_COLS)
    dc = np.clip(kc[None, :] - cc[:, None] + NA_COLS - 1, 0, 2 * NA_COLS - 2)
    n_dr, n_dc = 2 * NA_ROWS - 1, 2 * NA_COLS - 1
    sel = np.zeros((n_dc, GRID_W * GRID_W), np.float32)
    sel[dc.ravel(), np.arange(GRID_W * GRID_W)] = inwin.ravel()
    t2 = jnp.dot(rpb.astype(F32).reshape(C_HEADS * n_dr, n_dc), jnp.asarray(sel), precision=lax.Precision.HIGHEST)
    t2 = t2.reshape(C_HEADS, n_dr, GRID_W, GRID_W) + jnp.asarray(np.where(inwin, 0.0, NEG_BIG), F32)
    t = jnp.stack([t2[:, oi:oi + NA_ROWS] for oi in range(NA_ROWS)], axis=0)
    t = jnp.transpose(t, (0, 1, 3, 2, 4))
    return t.reshape(NA_ROWS, C_HEADS // 2, 2 * GRID_W, NA_WIN)


def _mha_kernel(q_ref, k_ref, v_ref, o_ref):
    n = q_ref.shape[0]
    lane_row = lax.broadcasted_iota(jnp.int32, (1, LANES), 1)
    lane = lax.broadcasted_iota(jnp.int32, (n, LANES), 1)
    for sl in range(C_W // LANES):
        cols = slice(sl * LANES, (sl + 1) * LANES)
        s = _dot_nt(_pair_lhs(q_ref[:, cols], lane_row), k_ref[:, cols])
        p = jnp.exp(s - jnp.max(s, axis=-1, keepdims=True))
        l = jnp.sum(p, axis=-1, keepdims=True)
        o = _dot(p.astype(BF16), v_ref[:, cols]) / l
        o_ref[:, cols] = jnp.where(lane < HALF, o[:n], o[n:]).astype(BF16)


def _mha_small(q, k, v):
    return pl.pallas_call(
        _mha_kernel,
        grid=(1,),
        in_specs=[_full(q.shape), _full(k.shape), _full(v.shape)],
        out_specs=_full(q.shape),
        out_shape=jax.ShapeDtypeStruct(q.shape, BF16),
        compiler_params=_cparams(1, 32),
        name="mha_small",
    )(q, k, v)


def _conv_kernel(y_ref, yp_ref, yn_ref, dw_ref, dwb_ref, lng_ref, lnb_ref, o_ref, ext, shifted, *, ts):
    i = pl.program_id(0)
    nb = pl.num_programs(0)
    ext[0:CONV_HALO] = jnp.where(i > 0, yp_ref[...], 0.0)
    ext[CONV_HALO:CONV_HALO + ts] = y_ref[...]
    ext[CONV_HALO + ts:] = jnp.where(i < nb - 1, yn_ref[...], 0.0)
    base = CONV_HALO - CONV_W // 2
    span = ts + (base + CONV_W - 1) // SUBLANES * SUBLANES
    for s in range(SUBLANES):
        shifted[s] = ext[s:s + span, :]
    acc = jnp.zeros((ts, D_CH), F32)
    for j in range(CONV_W):
        s, start = (base + j) % SUBLANES, (base + j) // SUBLANES * SUBLANES
        acc = acc + shifted[s, start:start + ts, :] * dw_ref[j:j + 1, :]
    yb = acc + dwb_ref[...]
    mu = jnp.mean(yb, axis=-1, keepdims=True)
    var = jnp.mean(jnp.square(yb - mu), axis=-1, keepdims=True)
    z = (yb - mu) * lax.rsqrt(var + EPS) * lng_ref[...] + lnb_ref[...]
    o_ref[...] = (z * jax.nn.sigmoid(z)).astype(BF16)


def _conv_module(y, dw, dwb, lng, lnb, ts):
    n = y.shape[0]
    hb = ts // CONV_HALO
    nh = n // CONV_HALO
    vec = pl.BlockSpec((1, D_CH), lambda i: (0, 0))
    return pl.pallas_call(
        functools.partial(_conv_kernel, ts=ts),
        grid=(n // ts,),
        in_specs=[pl.BlockSpec((ts, D_CH), lambda i: (i, 0)),
                  pl.BlockSpec((CONV_HALO, D_CH), lambda i: (jnp.maximum(i * hb - 1, 0), 0)),
                  pl.BlockSpec((CONV_HALO, D_CH), lambda i: (jnp.minimum((i + 1) * hb, nh - 1), 0)),
                  _full((CONV_W + 1, D_CH)), vec, vec, vec],
        out_specs=pl.BlockSpec((ts, D_CH), lambda i: (i, 0)),
        out_shape=jax.ShapeDtypeStruct((n, D_CH), BF16),
        scratch_shapes=[pltpu.VMEM((ts + 2 * CONV_HALO, D_CH), F32),
                        pltpu.VMEM((SUBLANES, ts + 2 * CONV_HALO - SUBLANES, D_CH), F32)],
        compiler_params=_cparams(1, 48),
        name="conv_module",
    )(y, y, y, dw, dwb, lng, lnb)


def _first_max2(vals):
    def first_max(vs):
        m = functools.reduce(jnp.maximum, vs)
        idx = len(vs) - 1
        for k in range(len(vs) - 2, -1, -1):
            idx = jnp.where(vs[k] == m, k, idx)
        return m, idx

    m1, i1 = first_max(vals)
    m2, i2 = first_max([jnp.where(i1 == k, -jnp.inf, v) for k, v in enumerate(vals)])
    return m1, m2, i1, i2


def _pick(idx, vals):
    out = vals[-1]
    for k in range(len(vals) - 2, -1, -1):
        out = jnp.where(idx == k, vals[k], out)
    return out


def _route_t(lt, rbc, tri, carry, ts):
    nk = EXPERTS_PER_GROUP
    aff = jax.nn.sigmoid(lt[:nk * SUBLANES])
    sel = aff + rbc[:nk * SUBLANES]
    sel_k = [sel[k * SUBLANES:(k + 1) * SUBLANES] for k in range(nk)]
    aff_k = [aff[k * SUBLANES:(k + 1) * SUBLANES] for k in range(nk)]
    m1, m2, _, _ = _first_max2(sel_k)
    gscore = m1 + m2
    best, bg = gscore[0:1], jnp.zeros((1, ts), jnp.int32)
    for g in range(1, N_GROUPS):
        upd = gscore[g:g + 1] > best
        best = jnp.where(upd, gscore[g:g + 1], best)
        bg = jnp.where(upd, g, bg)
    sel_b = [_pick(bg, [v[g:g + 1] for g in range(N_GROUPS)]) for v in sel_k]
    aff_b = [_pick(bg, [v[g:g + 1] for g in range(N_GROUPS)]) for v in aff_k]
    _, _, i1, i2 = _first_max2(sel_b)
    a1, a2 = _pick(i1, aff_b), _pick(i2, aff_b)
    e1 = bg * nk + i1
    e2 = bg * nk + i2
    erow = lax.broadcasted_iota(jnp.int32, (N_EXPERTS, ts), 0)
    hit1 = erow == e1
    hit2 = erow == e2
    onehot = jnp.where(hit1 | hit2, 1.0, 0.0)
    before = _dot(onehot.astype(BF16), tri) + carry[:, 0:1]
    r1 = jnp.sum(jnp.where(hit1, before, 0.0), axis=0, keepdims=True)
    r2 = jnp.sum(jnp.where(hit2, before, 0.0), axis=0, keepdims=True)
    new_carry = carry + jnp.sum(onehot, axis=1, keepdims=True)
    tot = a1 + a2
    rows = [e1.astype(F32), e2.astype(F32), a1 / tot, a2 / tot, r1, r2, jnp.zeros((2, ts), F32)]
    return jnp.concatenate(rows, axis=0), new_carry


def _post_mix_kernel(x_ref, a_ref, b_ref, w_ref, g1_ref, ng_ref, sc_ref, sh_ref, rwh_ref, rwl_ref, rb_ref, tri_ref,
                     xo_ref, h2_ref, route_ref, cnt_ref, *, ts):
    @pl.when(pl.program_id(0) == 0)
    def _():
        cnt_ref[...] = jnp.zeros_like(cnt_ref)

    half = w_ref.shape[0] // 2
    mo = _dot(a_ref[...], w_ref[:half]) + _dot(b_ref[...], w_ref[half:])
    xn = x_ref[...] + g1_ref[...] * mo
    xo_ref[...] = xn
    h2 = _rms_mod(xn, ng_ref[...], sc_ref[...], sh_ref[...])
    _store_row_tiles(h2_ref, h2, ts)
    hh, hl = _split_bf16(h2)
    lt = _dot_nt(rwh_ref[...], hh) + _dot_nt(rwh_ref[...], hl) + _dot_nt(rwl_ref[...], hh)
    rt, cnt = _route_t(lt, rb_ref[...], tri_ref[...], cnt_ref[...], ts)
    cnt_ref[...] = cnt
    route_ref[...] = jnp.concatenate([rt, jnp.zeros((LANES - SUBLANES, ts), F32)], axis=0).T


def _post_mix(x, a, b, w_out, mods, router, ts):
    n = x.shape[0]
    g1, ng, sc, sh = mods
    rwh, rwl, rbc = router
    tri = jnp.asarray(np.triu(np.ones((ts, ts), np.float32), 1), dtype=BF16)
    row = lambda i: (i, 0)
    vec = lambda w: pl.BlockSpec((1, w), lambda i: (0, 0))
    return pl.pallas_call(
        functools.partial(_post_mix_kernel, ts=ts),
        grid=(n // ts,),
        in_specs=[pl.BlockSpec((ts, D_MODEL), row), pl.BlockSpec((ts, a.shape[1]), row),
                  pl.BlockSpec((ts, b.shape[1]), row), _full(w_out.shape),
                  vec(D_MODEL), vec(D_MODEL), vec(D_MODEL), vec(D_MODEL),
                  _full(rwh.shape), _full(rwl.shape), _full(rbc.shape), _full(tri.shape)],
        out_specs=[pl.BlockSpec((ts, D_MODEL), row), pl.BlockSpec((ts * SUBLANES, LANES), row),
                   pl.BlockSpec((ts, LANES), row), _full((N_EXPERTS, LANES))],
        out_shape=[jax.ShapeDtypeStruct((n, D_MODEL), F32), jax.ShapeDtypeStruct((n * SUBLANES, LANES), F32),
                   jax.ShapeDtypeStruct((n, LANES), F32), jax.ShapeDtypeStruct((N_EXPERTS, LANES), F32)],
        compiler_params=_cparams(1, 56),
        name="post_mix",
    )(x, a, b, w_out, g1, ng, sc, sh, rwh, rwl, rbc, tri)


def _tile_copies(make_copy, ts):
    def issue(i, carry):
        for u in range(DMA_UNROLL):
            make_copy(i * DMA_UNROLL + u, 0).start(priority=0)
            make_copy(i * DMA_UNROLL + u, 1).start(priority=1)
        return carry

    def drain(i, carry):
        for u in range(DMA_UNROLL):
            make_copy(i * DMA_UNROLL + u, 0).wait()
            make_copy(i * DMA_UNROLL + u, 1).wait()
        return carry

    lax.fori_loop(0, ts // DMA_UNROLL, issue, 0)
    lax.fori_loop(0, ts // DMA_UNROLL, drain, 0)


def _dispatch_kernel(pend_ref, dest_ref, h_ref, buf_ref, zeros_scr, sem, zsem, *, ts):
    @pl.when(pl.program_id(0) == 0)
    def _():
        zeros_scr[...] = jnp.zeros_like(zeros_scr)

        def zero_copy(e):
            start = pl.multiple_of((pend_ref[e] - MOE_BLOCK) * SUBLANES, SUBLANES)
            return pltpu.make_async_copy(zeros_scr, buf_ref.at[pl.ds(start, MOE_BLOCK * SUBLANES)], zsem)

        def nonempty(e):
            return pend_ref[e] > (pend_ref[e - 1] if e > 0 else 0)

        for e in range(N_EXPERTS):
            pl.when(nonempty(e))(lambda e=e: zero_copy(e).start())
        for e in range(N_EXPERTS):
            pl.when(nonempty(e))(lambda e=e: zero_copy(e).wait())

        def tail_copy(b):
            start = pl.multiple_of(b * (MOE_BLOCK * SUBLANES), MOE_BLOCK * SUBLANES)
            return pltpu.make_async_copy(zeros_scr, buf_ref.at[pl.ds(start, MOE_BLOCK * SUBLANES)], zsem)

        n_used = pend_ref[N_EXPERTS - 1] // MOE_BLOCK
        n_blocks = buf_ref.shape[0] // (MOE_BLOCK * SUBLANES)
        lax.fori_loop(n_used, n_blocks, lambda b, c: (tail_copy(b).start(), c)[1], 0)
        lax.fori_loop(n_used, n_blocks, lambda b, c: (tail_copy(b).wait(), c)[1], 0)

    def row_copy(r, k):
        src = h_ref.at[pl.ds(pl.multiple_of(r * SUBLANES, SUBLANES), SUBLANES)]
        dst = buf_ref.at[pl.ds(pl.multiple_of(dest_ref[0, 0, 2 * r + k], SUBLANES), SUBLANES)]
        return pltpu.make_async_copy(src, dst, sem)

    _tile_copies(row_copy, ts)


def _dispatch(h2t, dest3, pend, n_rows, ts):
    n = h2t.shape[0] // SUBLANES
    grid_spec = pltpu.PrefetchScalarGridSpec(
        num_scalar_prefetch=1,
        grid=(n // ts,),
        in_specs=[pl.BlockSpec((1, 1, 2 * ts), lambda i, pe: (i, 0, 0), memory_space=pltpu.SMEM),
                  pl.BlockSpec((ts * SUBLANES, LANES), lambda i, pe: (i, 0))],
        out_specs=pl.BlockSpec(memory_space=pl.ANY),
        scratch_shapes=[pltpu.VMEM((MOE_BLOCK * SUBLANES, LANES), F32), pltpu.SemaphoreType.DMA(()),
                        pltpu.SemaphoreType.DMA(())],
    )
    return pl.pallas_call(
        functools.partial(_dispatch_kernel, ts=ts),
        grid_spec=grid_spec,
        out_shape=jax.ShapeDtypeStruct((n_rows * SUBLANES, LANES), F32),
        compiler_params=_cparams(1, 32),
        name="moe_dispatch",
    )(pend, dest3, h2t)


def _expert_kernel(blk_e_ref, n_used_ref, x_ref, wg_ref, wu_ref, wd_ref, y_ref, wg_s, wu_s, wd_s):
    i = pl.program_id(0)

    @pl.when((i == 0) | (blk_e_ref[i] != blk_e_ref[jnp.maximum(i - 1, 0)]))
    def _():
        wg_s[...] = wg_ref[0, 0].astype(BF16)
        wu_s[...] = wu_ref[0, 0].astype(BF16)
        wd_s[...] = wd_ref[0, 0].astype(BF16)

    @pl.when(i < n_used_ref[0])
    def _():
        xb = _load_row_tiles(x_ref, MOE_BLOCK).astype(BF16)
        gate = _dot(xb, wg_s[...])
        up = _dot(xb, wu_s[...])
        hmid = gate * jax.nn.sigmoid(gate) * up
        _store_row_tiles(y_ref, _dot(hmid.astype(BF16), wd_s[...]), MOE_BLOCK)

    @pl.when(i >= n_used_ref[0])
    def _():
        y_ref[...] = jnp.zeros_like(y_ref)


def _experts(buf, blk_e, n_used, layer, wg, wu, wd):
    nb = buf.shape[0] // (MOE_BLOCK * SUBLANES)
    used = lambda i, be, nu: (jnp.minimum(i, nu[0] - 1), 0)
    wsel = lambda i, be, nu: (layer, be[i], 0, 0)
    grid_spec = pltpu.PrefetchScalarGridSpec(
        num_scalar_prefetch=2,
        grid=(nb,),
        in_specs=[pl.BlockSpec((MOE_BLOCK * SUBLANES, LANES), used),
                  pl.BlockSpec((1, 1, D_MODEL, D_EXPERT), wsel),
                  pl.BlockSpec((1, 1, D_MODEL, D_EXPERT), wsel),
                  pl.BlockSpec((1, 1, D_EXPERT, D_MODEL), wsel)],
        out_specs=pl.BlockSpec((MOE_BLOCK * SUBLANES, LANES), lambda i, be, nu: (i, 0)),
        scratch_shapes=[pltpu.VMEM((D_MODEL, D_EXPERT), BF16), pltpu.VMEM((D_MODEL, D_EXPERT), BF16),
                        pltpu.VMEM((D_EXPERT, D_MODEL), BF16)],
    )
    return pl.pallas_call(
        _expert_kernel,
        grid_spec=grid_spec,
        out_shape=jax.ShapeDtypeStruct(buf.shape, F32),
        compiler_params=_cparams(1, 48),
        name="moe_experts",
    )(blk_e, n_used, buf, wg, wu, wd)


def _combine_kernel(dest_ref, x_ref, route_ref, g2_ref, y_ref, o_ref, g0, g1, sem, *, ts):
    def row_copy(r, k):
        src = y_ref.at[pl.ds(pl.multiple_of(dest_ref[0, 0, 2 * r + k], SUBLANES), SUBLANES)]
        dst = (g0 if k == 0 else g1).at[pl.ds(pl.multiple_of(r * SUBLANES, SUBLANES), SUBLANES)]
        return pltpu.make_async_copy(src, dst, sem)

    _tile_copies(row_copy, ts)
    route = route_ref[...]
    w0, w1 = route[:, 2:3], route[:, 3:4]
    for k in range(SUBLANES):
        cols = slice(k * LANES, (k + 1) * LANES)
        f = g0[pl.ds(k, ts, stride=SUBLANES), :] * w0 + g1[pl.ds(k, ts, stride=SUBLANES), :] * w1
        o_ref[:, cols] = x_ref[:, cols] + g2_ref[:, cols] * f


def _combine(x, route, g2, y, dest3, ts):
    n = x.shape[0]
    row = lambda i: (i, 0)
    return pl.pallas_call(
        functools.partial(_combine_kernel, ts=ts),
        grid=(n // ts,),
        in_specs=[pl.BlockSpec((1, 1, 2 * ts), lambda i: (i, 0, 0), memory_space=pltpu.SMEM),
                  pl.BlockSpec((ts, D_MODEL), row), pl.BlockSpec((ts, LANES), row),
                  pl.BlockSpec((1, D_MODEL), lambda i: (0, 0)),
                  pl.BlockSpec(memory_space=pl.ANY)],
        out_specs=pl.BlockSpec((ts, D_MODEL), row),
        out_shape=jax.ShapeDtypeStruct((n, D_MODEL), F32),
        scratch_shapes=[pltpu.VMEM((ts * SUBLANES, LANES), F32), pltpu.VMEM((ts * SUBLANES, LANES), F32),
                        pltpu.SemaphoreType.DMA(())],
        compiler_params=_cparams(1, 48),
        name="moe_combine",
    )(dest3, x, route, g2, y)


def _moe(x, h2, route, counts, g2, layer, wg, wu, wd, ts):
    n = x.shape[0]
    experts = jnp.arange(N_EXPERTS, dtype=jnp.int32)
    cnt = counts[:, 0].astype(jnp.int32)
    pcounts = (cnt + MOE_BLOCK - 1) // MOE_BLOCK * MOE_BLOCK
    pend = jnp.cumsum(pcounts)
    pstart = pend - pcounts
    e = route[:, 0:2].astype(jnp.int32)
    rank = route[:, 4:6].astype(jnp.int32)
    dest = jnp.sum(jnp.where(e[:, :, None] == experts, pstart, 0), axis=-1) + rank
    n_blocks = -(-2 * n // MOE_BLOCK) + N_EXPERTS
    blk_pos = jnp.arange(n_blocks, dtype=jnp.int32) * MOE_BLOCK
    blk_e = jnp.sum((pend[None, :] <= blk_pos[:, None]).astype(jnp.int32), axis=1)
    used = blk_pos < pend[-1]
    blk_e = jnp.where(used, blk_e, jnp.max(jnp.where(used, blk_e, 0)))
    n_used = (pend[-1:] // MOE_BLOCK).astype(jnp.int32)
    dest3 = (dest * SUBLANES).reshape(n // ts, 1, 2 * ts)
    buf = _dispatch(h2, dest3, pend.astype(jnp.int32), n_blocks * MOE_BLOCK, ts)
    y = _experts(buf, blk_e, n_used, layer, wg, wu, wd)
    return _combine(x, route, g2, y, dest3, ts)


def _block_diag_ones(width):
    idx = np.arange(width) // HALF
    return jnp.asarray((idx[:, None] == idx[None, :]).astype(np.float32), dtype=BF16)


def _rope_tables(n_tok):
    t = jnp.arange(n_tok, dtype=jnp.int32)
    pos = jnp.stack([t // GRID_W, t % GRID_W], axis=-1).astype(F32)
    inv = ROPE_THETA ** (-jnp.arange(ROPE_FREQS, dtype=F32) / ROPE_FREQS)
    ang = pos[:, :, None] * inv
    cos, sin = jnp.cos(ang), jnp.sin(ang)
    c = jnp.concatenate([cos, cos], axis=-1).reshape(n_tok, HEAD_DIM)
    s = jnp.concatenate([-sin, sin], axis=-1).reshape(n_tok, HEAD_DIM)
    return jnp.tile(c, (1, 2)), jnp.tile(s, (1, 2))


def _tile_vec(v, reps):
    return jnp.tile(v.astype(F32), reps).reshape(1, -1)


def kernel(x, c, ctx, c_ctx, router_w, router_b, ada_w, ada_b, norm1_g, norm2_g, ev_w_in, ev_w_out, a_q_norm,
           a_k_norm, b_v_norm, b_ws, b_bs, od_w_in, od_w_out, c_q_norm, c_k_norm, c_rpb, d_dw, d_dw_b, d_ln_g,
           d_ln_b, moe_w_gate, moe_w_up, moe_w_down):
    assert x.shape[0] == 1 and ctx.shape[0] == 1
    s_len, n_ctx = x.shape[1], ctx.shape[1]
    ts_lat = min(512, s_len)
    ts_ctx = n_ctx
    x_lat = x[0]
    x_ctx = ctx[0]

    cc = jnp.zeros((8, D_MODEL), F32).at[0].set(c[0]).at[1].set(c_ctx)
    mods = _modulation(cc, ada_w, ada_b)

    def mod(l, row, k):
        return mods[l, row, k * D_MODEL:(k + 1) * D_MODEL].reshape(1, D_MODEL)

    bd512 = _block_diag_ones(A_Q)
    bd128 = _block_diag_ones(LANES)
    cos_lat, sin_lat = _rope_tables(s_len)
    cos_ctx, sin_ctx = jnp.ones((n_ctx, LANES), F32), jnp.zeros((n_ctx, LANES), F32)
    e_of_row = np.arange(N_EXPERTS)
    row_of_e = (e_of_row % EXPERTS_PER_GROUP) * SUBLANES + e_of_row // EXPERTS_PER_GROUP
    rw = jnp.zeros((LANES, D_MODEL), F32).at[row_of_e].set(router_w.T)
    rwh = rw.astype(BF16)
    rwl = (rw - rwh.astype(F32)).astype(BF16)
    rbc = jnp.zeros((LANES, 1), F32).at[row_of_e, 0].set(router_b)
    router = (rwh, rwl, rbc)

    for l in range(DEPTH):
        need_ctx = l < DEPTH - 1
        i = l // 2
        g_n1 = norm1_g[l].reshape(1, D_MODEL)
        g_n2 = norm2_g[l].reshape(1, D_MODEL)
        m_lat = [mod(l, 0, k) for k in range(6)]
        m_ctx = [mod(l, 1, k) for k in range(6)]
        if l % 2 == 0:
            w_in = ev_w_in[i].astype(BF16)
            w_out = ev_w_out[i].astype(BF16)
            consts = (bd512, bd128, _tile_vec(a_q_norm[i], A_HEADS), _tile_vec(a_k_norm[i], A_KV_HEADS),
                      b_v_norm[i].reshape(1, B_WIDTH).astype(F32), b_ws[i].astype(BF16),
                      jnp.repeat(b_bs[i].T, HALF, axis=1))
            q_l, kd_l, v_l, b_l = _proj_even(x_lat, (g_n1, m_lat[1], m_lat[0]), w_in, consts, cos_lat, sin_lat, ts_lat)
            q_c, kd_c, v_c, b_c = _proj_even(x_ctx, (g_n1, m_ctx[1], m_ctx[0]), w_in, consts, cos_ctx, sin_ctx, ts_ctx)
            kd_cc = jnp.concatenate([kd_c, kd_c], axis=2)
            v_cc = jnp.concatenate([v_c, jnp.zeros_like(v_c)], axis=1)
            kd_all = jnp.concatenate([kd_l, kd_cc], axis=2)
            v_all = jnp.concatenate([v_l, v_cc], axis=1)
            mix_l = (_attn_even(q_l, kd_all, v_all, min(ATTN_TQ, s_len), ATTN_TK), b_l)
            if need_ctx:
                mix_c = (_attn_even(q_c, kd_cc, v_cc, ATTN_TQ, ATTN_TK), b_c)
        else:
            w_in = od_w_in[i].astype(BF16)
            w_out = od_w_out[i].astype(BF16)
            consts = (bd512, _tile_vec(c_q_norm[i], C_HEADS), _tile_vec(c_k_norm[i], C_HEADS))
            q_l, k_l, v_l, y_l = _proj_odd(x_lat, (g_n1, m_lat[1], m_lat[0]), w_in, consts, ts_lat)
            q_c, k_c, v_c, y_c = _proj_odd(x_ctx, (g_n1, m_ctx[1], m_ctx[0]), w_in, consts, ts_ctx)
            dw = jnp.zeros((CONV_W + 1, D_CH), F32).at[:CONV_W].set(d_dw[i])
            conv_p = (dw, d_dw_b[i].reshape(1, D_CH), d_ln_g[i].reshape(1, D_CH), d_ln_b[i].reshape(1, D_CH))
            mix_l = (_na_attention(q_l, k_l, v_l, k_c, v_c, _na_bias(c_rpb[i])), _conv_module(y_l, *conv_p, ts_lat))
            if need_ctx:
                mix_c = (_mha_small(q_c, k_c, v_c), _conv_module(y_c, *conv_p, ts_ctx))
        moe_w = (l, moe_w_gate, moe_w_up, moe_w_down)
        xn, h2, route, cnt = _post_mix(x_lat, *mix_l, w_out, (m_lat[2], g_n2, m_lat[4], m_lat[3]), router, ts_lat)
        x_lat = _moe(xn, h2, route, cnt, m_lat[5], *moe_w, ts_lat)
        if need_ctx:
            xn, h2, route, cnt = _post_mix(x_ctx, *mix_c, w_out, (m_ctx[2], g_n2, m_ctx[4], m_ctx[3]), router, ts_ctx)
            x_ctx = _moe(xn, h2, route, cnt, m_ctx[5], *moe_w, ts_ctx)
    return x_lat[None]
```

```python
import functools

import numpy as np
import jax
import jax.numpy as jnp
from jax import lax
from jax.experimental import pallas as pl
from jax.experimental.pallas import tpu as pltpu

F32 = jnp.float32
BF16 = jnp.bfloat16

D_MODEL = 1024
DEPTH = 4
GRID_W = 64
HEAD_DIM = 64
EPS = 1e-6
A_HEADS = 8
A_KV_HEADS = 2
A_Q = A_HEADS * HEAD_DIM
A_KV = A_KV_HEADS * HEAD_DIM
A_IN = A_Q + 2 * A_KV
ROPE_THETA = 10000.0
ROPE_FREQS = HEAD_DIM // 4
B_GROUPS = 8
B_WIDTH = 512
CHUNK = 128
C_HEADS = 8
C_W = C_HEADS * HEAD_DIM
NA_ROWS = 8
NA_COLS = 16
D_CH = 512
CONV_W = 31
EVEN_IN = A_IN + 2 * B_WIDTH
ODD_IN = 3 * C_W + 2 * D_CH
N_EXPERTS = 16
N_GROUPS = 4
EXPERTS_PER_GROUP = 4
D_EXPERT = 512
MOE_BLOCK = 256

LANES = 128
SUBLANES = 8
HALF = 64
NEG_BIG = -1e30
CONV_HALO = 16
DMA_UNROLL = 8
ATTN_TQ = 256
ATTN_TK = 512
QK_SCALE_LOG2 =HEAD_DIM ** -0.5 * 1.4426950408889634


def _cparams(n_axes, vmem_mb):
    return pltpu.CompilerParams(dimension_semantics=("arbitrary",) * n_axes,
                                vmem_limit_bytes=vmem_mb << 20)


def _full(shape):
    nd = len(shape)
    return pl.BlockSpec(shape, lambda *_: (0,) * nd)


def _dot(a, b):
    return jnp.dot(a, b, preferred_element_type=F32)


def _dot_nt(a, b):
    return lax.dot_general(a, b, (((1,), (1,)), ((), ())), preferred_element_type=F32)


def _split_bf16(x):
    hi = x.astype(BF16)
    lo = (x - hi.astype(F32)).astype(BF16)
    return hi, lo


def _seg_mean_sq(x, bd):
    hi, lo = _split_bf16(x * x)
    return (_dot(hi, bd) + _dot(lo, bd)) * (1.0 / HALF)


def _rms_mod(x, g, sc, sh):
    ms = jnp.mean(x * x, axis=-1, keepdims=True)
    return x * lax.rsqrt(ms + EPS) * g * (1.0 + sc) + sh


def _rope(x, c, sg, lane):
    sw = jnp.where((lane % 32) < 16, pltpu.roll(x, LANES - 16, 1), pltpu.roll(x, 16, 1))
    return x * c + sw * sg


def _pair_lhs(qs, lane_row):
    ma = jnp.where(lane_row < HALF, 1.0, 0.0).astype(BF16)
    mb = jnp.where(lane_row < HALF, 0.0, 1.0).astype(BF16)
    return jnp.concatenate([qs * ma, qs * mb], axis=0)


def _store_row_tiles(ref, val, t, row0=0):
    for k in range(SUBLANES):
        ref[pl.ds(row0 * SUBLANES + k, t, stride=SUBLANES), :] = val[:, k * LANES:(k + 1) * LANES]


def _load_row_tiles(ref, t, row0=0):
    return jnp.concatenate([ref[pl.ds(row0 * SUBLANES + k, t, stride=SUBLANES), :] for k in range(SUBLANES)], axis=1)


def _mod_kernel(cc_ref, w_ref, b_ref, o_ref):
    cc = cc_ref[...]
    s = cc * jax.nn.sigmoid(cc)
    sh, sl = _split_bf16(s)
    wh, wl = _split_bf16(w_ref[0])
    o_ref[0] = _dot(sh, wh) + _dot(sl, wh) + _dot(sh, wl) + b_ref[0]


def _modulation(cc, ada_w, ada_b):
    tn = 1024
    return pl.pallas_call(
        _mod_kernel,
        grid=(DEPTH, 6 * D_MODEL // tn),
        in_specs=[pl.BlockSpec((8, D_MODEL), lambda l, j: (0, 0)),
                  pl.BlockSpec((1, D_MODEL, tn), lambda l, j: (l, 0, j)),
                  pl.BlockSpec((1, 1, tn), lambda l, j: (l, 0, j))],
        out_specs=pl.BlockSpec((1, 8, tn), lambda l, j: (l, 0, j)),
        out_shape=jax.ShapeDtypeStruct((DEPTH, 8, 6 * D_MODEL), F32),
        compiler_params=_cparams(2, 32),
        name="modulation",
    )(cc, ada_w, ada_b.reshape(DEPTH, 1, 6 * D_MODEL))


def _proj_even_kernel(x_ref, g_ref, sc_ref, sh_ref, w_ref, bd512_ref, bd128_ref, qg_ref, kg_ref, vg_ref,
                      cos_ref, sin_ref, ws_ref, bsb_ref, q_ref, kd_ref, v_ref, b_ref, *, ts):
    h = _rms_mod(x_ref[...], g_ref[...], sc_ref[...], sh_ref[...])
    p = _dot(h.astype(BF16), w_ref[...])
    lane = lax.broadcasted_iota(jnp.int32, (ts, LANES), 1)
    c = cos_ref[...]
    sg = sin_ref[...]

    q = p[:, :A_Q]
    qn = q * lax.rsqrt(_seg_mean_sq(q, bd512_ref[...]) + EPS) * qg_ref[...]
    for sl in range(A_Q // LANES):
        xs = qn[:, sl * LANES:(sl + 1) * LANES]
        q_ref[:, sl * LANES:(sl + 1) * LANES] = (_rope(xs, c, sg, lane) * QK_SCALE_LOG2).astype(BF16)

    k = p[:, A_Q:A_Q + A_KV]
    kn = k * lax.rsqrt(_seg_mean_sq(k, bd128_ref[...]) + EPS) * kg_ref[...]
    kt = _rope(kn, c, sg, lane).T
    for j in range(A_KV_HEADS):
        kj = kt[j * HALF:(j + 1) * HALF]
        kd_ref[j] = jnp.concatenate([kj, kj], axis=0).astype(BF16)
    v = p[:, A_Q + A_KV:A_IN]
    v_ref[0] = jnp.where(lane < HALF, v, 1.0).astype(BF16)
    v_ref[1] = jnp.where(lane < HALF, pltpu.roll(v, HALF, 1), 1.0).astype(BF16)

    zu = jax.nn.gelu(p[:, A_IN:A_IN + B_WIDTH])
    zv = jax.nn.gelu(p[:, A_IN + B_WIDTH:])
    vn = (zv * lax.rsqrt(_seg_mean_sq(zv, bd512_ref[...]) + EPS) * vg_ref[...]).astype(BF16)
    lane_c = lax.broadcasted_iota(jnp.int32, (CHUNK, LANES), 1)
    for ch in range(ts // CHUNK):
        rows = slice(ch * CHUNK, (ch + 1) * CHUNK)
        for sl in range(B_WIDTH // LANES):
            cols = slice(sl * LANES, (sl + 1) * LANES)
            vs = vn[rows, cols]
            mixed = jnp.where(lane_c < HALF, _dot(ws_ref[2 * sl], vs), _dot(ws_ref[2 * sl + 1], vs))
            b_ref[rows, cols] = (zu[rows, cols] * (mixed + bsb_ref[:, cols])).astype(BF16)


def _proj_even(x, mods, w_in, consts, cos, sin, ts):
    n = x.shape[0]
    g, sc, sh = mods
    bd512, bd128, qg, kg, vg, ws, bsb = consts
    row = lambda i: (i, 0)
    vec = lambda w: pl.BlockSpec((1, w), lambda i: (0, 0))
    return pl.pallas_call(
        functools.partial(_proj_even_kernel, ts=ts),
        grid=(n // ts,),
        in_specs=[pl.BlockSpec((ts, D_MODEL), row), vec(D_MODEL), vec(D_MODEL), vec(D_MODEL),
                  _full((D_MODEL, EVEN_IN)), _full((A_Q, A_Q)), _full((LANES, LANES)),
                  vec(A_Q), vec(A_KV), vec(B_WIDTH),
                  pl.BlockSpec((ts, LANES), row), pl.BlockSpec((ts, LANES), row),
                  _full((B_GROUPS, CHUNK, CHUNK)), _full((CHUNK, B_WIDTH))],
        out_specs=[pl.BlockSpec((ts, A_Q), row),
                   pl.BlockSpec((A_KV_HEADS, LANES, ts), lambda i: (0, 0, i)),
                   pl.BlockSpec((A_KV_HEADS, ts, LANES), lambda i: (0, i, 0)),
                   pl.BlockSpec((ts, B_WIDTH), row)],
        out_shape=[jax.ShapeDtypeStruct((n, A_Q), BF16),
                   jax.ShapeDtypeStruct((A_KV_HEADS, LANES, n), BF16),
                   jax.ShapeDtypeStruct((A_KV_HEADS, n, LANES), BF16),
                   jax.ShapeDtypeStruct((n, B_WIDTH), BF16)],
        compiler_params=_cparams(1, 56),
        name="proj_even",
    )(x, g, sc, sh, w_in, bd512, bd128, qg, kg, vg, cos, sin, ws, bsb)


def _attn_even_kernel(q_ref, kd_ref, va_ref, o_ref, lhs_scr, s_a, s_b, mx_a, mx_b, m_scr, acc_scr,
                      *, tq, tk, n_chunks):
    rows = 4 * tq
    lane_row = lax.broadcasted_iota(jnp.int32, (1, LANES), 1)
    lane = lax.broadcasted_iota(jnp.int32, (tq, LANES), 1)
    for j in range(A_KV_HEADS):
        lhs_scr[...] = jnp.concatenate([_pair_lhs(q_ref[:, sl * LANES:(sl + 1) * LANES], lane_row)
                                        for sl in (2 * j, 2 * j + 1)], axis=0)
        m_scr[...] = jnp.full((rows, LANES), NEG_BIG, F32)
        acc_scr[...] = jnp.zeros((rows, LANES), F32)

        def scores(c, s_scr, mx_scr, j=j):
            off = pl.multiple_of(c * tk, tk)
            s = _dot(lhs_scr[...], kd_ref[j, :, pl.ds(off, tk)])
            s_scr[...] = s
            mx_scr[...] = jnp.broadcast_to(jnp.max(s, axis=-1, keepdims=True), (rows, LANES))

        def accumulate(c, s_scr, mx_scr, j=j):
            off = pl.multiple_of(c * tk, tk)
            m_old = m_scr[...]
            m_new = jnp.maximum(m_old, mx_scr[...])
            p = jnp.concatenate([jnp.exp2(s_scr[:, t * LANES:(t + 1) * LANES] - m_new)
                                 for t in range(tk // LANES)], axis=1).astype(BF16)
            pv = _dot(p, va_ref[j, pl.ds(off, tk), :])
            acc_scr[...] = jnp.exp2(m_old - m_new) * acc_scr[...] + pv
            m_scr[...] = m_new

        scores(0, s_a, mx_a)

        def body(i, carry):
            scores(2 * i + 1, s_b, mx_b)
            accumulate(2 * i, s_a, mx_a)
            scores(2 * i + 2, s_a, mx_a)
            accumulate(2 * i + 1, s_b, mx_b)
            return carry

        lax.fori_loop(0, (n_chunks - 1) // 2, body, 0, unroll=2)
        accumulate(n_chunks - 1, s_a, mx_a)
        acc = acc_scr[...]
        o = acc / pltpu.roll(acc, HALF, 1)
        for t, sl in enumerate((2 * j, 2 * j + 1)):
            oa = o[(2 * t) * tq:(2 * t + 1) * tq]
            ob = o[(2 * t + 1) * tq:(2 * t + 2) * tq]
            o_ref[:, sl * LANES:(sl + 1) * LANES] = jnp.where(lane < HALF, oa, pltpu.roll(ob, HALF, 1)).astype(BF16)


def _attn_even(q, kd, va, tq, tk):
    n = q.shape[0]
    n_chunks = kd.shape[2] // tk
    assert n_chunks % 2 == 1
    rows = 4 * tq
    big = lambda w: pltpu.VMEM((rows, w), F32)
    return pl.pallas_call(
        functools.partial(_attn_even_kernel, tq=tq, tk=tk, n_chunks=n_chunks),
        grid=(n // tq,),
        in_specs=[pl.BlockSpec((tq, A_Q), lambda i: (i, 0)), _full(kd.shape), _full(va.shape)],
        out_specs=pl.BlockSpec((tq, A_Q), lambda i: (i, 0)),
        out_shape=jax.ShapeDtypeStruct((n, A_Q), BF16),
        scratch_shapes=[pltpu.VMEM((rows, LANES), BF16), big(tk), big(tk), big(LANES), big(LANES),
                        big(LANES), big(LANES)],
        compiler_params=_cparams(1, 56),
        name="attn_even",
    )(q, kd, va)


def _proj_odd_kernel(x_ref, g_ref, sc_ref, sh_ref, w_ref, bd512_ref, qg_ref, kg_ref,
                     q_ref, k_ref, v_ref, y_ref):
    h = _rms_mod(x_ref[...], g_ref[...], sc_ref[...], sh_ref[...])
    p = _dot(h.astype(BF16), w_ref[...])
    bd = bd512_ref[...]
    q = p[:, :C_W]
    q_ref[...] = (q * lax.rsqrt(_seg_mean_sq(q, bd) + EPS) * qg_ref[...] * (HEAD_DIM ** -0.5)).astype(BF16)
    k = p[:, C_W:2 * C_W]
    k_ref[...] = (k * lax.rsqrt(_seg_mean_sq(k, bd) + EPS) * kg_ref[...]).astype(BF16)
    v_ref[...] = p[:, 2 * C_W:3 * C_W].astype(BF16)
    y_ref[...] = p[:, 3 * C_W:3 * C_W + D_CH] * jax.nn.sigmoid(p[:, 3 * C_W + D_CH:])


def _proj_odd(x, mods, w_in, consts, ts):
    n = x.shape[0]
    g, sc, sh = mods
    bd512, qg, kg = consts
    row = lambda i: (i, 0)
    vec = lambda w: pl.BlockSpec((1, w), lambda i: (0, 0))
    blk = pl.BlockSpec((ts, C_W), row)
    return pl.pallas_call(
        _proj_odd_kernel,
        grid=(n // ts,),
        in_specs=[pl.BlockSpec((ts, D_MODEL), row), vec(D_MODEL), vec(D_MODEL), vec(D_MODEL),
                  _full((D_MODEL, ODD_IN)), _full((C_W, C_W)), vec(C_W), vec(C_W)],
        out_specs=[blk, blk, blk, blk],
        out_shape=[jax.ShapeDtypeStruct((n, C_W), BF16), jax.ShapeDtypeStruct((n, C_W), BF16),
                   jax.ShapeDtypeStruct((n, C_W), BF16), jax.ShapeDtypeStruct((n, D_CH), F32)],
        compiler_params=_cparams(1, 56),
        name="proj_odd",
    )(x, g, sc, sh, w_in, bd512, qg, kg)


NA_BLOCK_ROWS = 8
NA_TOK = NA_BLOCK_ROWS * GRID_W
NA_WIN = NA_ROWS * GRID_W


def _na_kernel(q_ref, kp_ref, kc_ref, kn_ref, vp_ref, vc_ref, vn_ref, kx_ref, vx_ref, bias_ref, o_ref,
               kbuf, vbuf, *, rows):
    i = pl.program_id(0)
    for t, (kr, vr) in enumerate(((kp_ref, vp_ref), (kc_ref, vc_ref), (kn_ref, vn_ref))):
        kbuf[t * NA_TOK:(t + 1) * NA_TOK] = kr[...]
        vbuf[t * NA_TOK:(t + 1) * NA_TOK] = vr[...]
    lane_row = lax.broadcasted_iota(jnp.int32, (1, LANES), 1)
    lane = lax.broadcasted_iota(jnp.int32, (GRID_W, LANES), 1)
    for sl in range(C_W // LANES):
        cols = slice(sl * LANES, (sl + 1) * LANES)
        lhs = _pair_lhs(q_ref[:, cols], lane_row)
        s_ctx = _dot_nt(lhs, kx_ref[:, cols])
        sa, sb, starts = [], [], []
        for j in range(NA_BLOCK_ROWS):
            r = i * NA_BLOCK_ROWS + j
            r0 = jnp.clip(r - NA_ROWS // 2, 0, rows - NA_ROWS)
            start = pl.multiple_of((r0 - (i - 1) * NA_BLOCK_ROWS) * GRID_W, GRID_W)
            starts.append(start)
            qa = lhs[j * GRID_W:(j + 1) * GRID_W]
            qb = lhs[NA_TOK + j * GRID_W:NA_TOK + (j + 1) * GRID_W]
            kw = kbuf[pl.ds(start, NA_WIN), cols]
            s = _dot_nt(jnp.concatenate([qa, qb], axis=0), kw) + bias_ref[r0 - r + NA_ROWS - 1, sl]
            sa.append(s[:GRID_W])
            sb.append(s[GRID_W:])
        s_loc = jnp.concatenate(sa + sb, axis=0)
        m = jnp.maximum(jnp.max(s_loc, axis=-1, keepdims=True), jnp.max(s_ctx, axis=-1, keepdims=True))
        p_loc = jnp.exp(s_loc - m)
        p_ctx = jnp.exp(s_ctx - m)
        l = jnp.sum(p_loc, axis=-1, keepdims=True) + jnp.sum(p_ctx, axis=-1, keepdims=True)
        o_ctx = _dot(p_ctx.astype(BF16), vx_ref[:, cols])
        p_loc = p_loc.astype(BF16)
        for j in range(NA_BLOCK_ROWS):
            ra = slice(j * GRID_W, (j + 1) * GRID_W)
            rb = slice(NA_TOK + j * GRID_W, NA_TOK + (j + 1) * GRID_W)
            vw = vbuf[pl.ds(starts[j], NA_WIN), cols]
            o2 = _dot(jnp.concatenate([p_loc[ra], p_loc[rb]], axis=0), vw)
            oa = (o2[:GRID_W] + o_ctx[ra]) / l[ra]
            ob = (o2[GRID_W:] + o_ctx[rb]) / l[rb]
            o_ref[ra, cols] = jnp.where(lane < HALF, oa, ob).astype(BF16)


def _na_attention(q, k, v, kx, vx, bias):
    n = q.shape[0]
    nb = n // NA_TOK
    blk = lambda f: pl.BlockSpec((NA_TOK, C_W), f)
    prev = lambda i: (jnp.maximum(i - 1, 0), 0)
    cur = lambda i: (i, 0)
    nxt = lambda i: (jnp.minimum(i + 1, nb - 1), 0)
    return pl.pallas_call(
        functools.partial(_na_kernel, rows=n // GRID_W),
        grid=(nb,),
        in_specs=[blk(cur), blk(prev), blk(cur), blk(nxt), blk(prev), blk(cur), blk(nxt),
                  _full(kx.shape), _full(vx.shape), _full(bias.shape)],
        out_specs=blk(cur),
        out_shape=jax.ShapeDtypeStruct((n, C_W), BF16),
        scratch_shapes=[pltpu.VMEM((3 * NA_TOK, C_W), BF16), pltpu.VMEM((3 * NA_TOK, C_W), BF16)],
        compiler_params=_cparams(1, 56),
        name="na_attention",
    )(q, k, k, k, v, v, v, kx, vx, bias)


def _na_bias(rpb):
    cc = np.arange(GRID_W)
    c0 = np.clip(cc - NA_COLS // 2, 0, GRID_W - NA_COLS)
    kc = np.arange(GRID_W)
    inwin = (kc[None, :] >= c0[:, None]) & (kc[None, :] < c0[:, None] + NA_COLS)
    dc = np.clip(kc[None, :] - cc[:, None] + NA_COLS - 1, 0, 2 * NA_COLS - 2)
    n_dr, n_dc = 2 * NA_ROWS - 1, 2 * NA_COLS - 1
    sel = np.zeros((n_dc, GRID_W * GRID_W), np.float32)
    sel[dc.ravel(), np.arange(GRID_W * GRID_W)] = inwin.ravel()
    t2 = jnp.dot(rpb.astype(F32).reshape(C_HEADS * n_dr, n_dc), jnp.asarray(sel), precision=lax.Precision.HIGHEST)
    t2 = t2.reshape(C_HEADS, n_dr, GRID_W, GRID_W) + jnp.asarray(np.where(inwin, 0.0, NEG_BIG), F32)
    t = jnp.stack([t2[:, oi:oi + NA_ROWS] for oi in range(NA_ROWS)], axis=0)
    t = jnp.transpose(t, (0, 1, 3, 2, 4))
    return t.reshape(NA_ROWS, C_HEADS // 2, 2 * GRID_W, NA_WIN)


def _mha_kernel(q_ref, k_ref, v_ref, o_ref):
    n = q_ref.shape[0]
    lane_row = lax.broadcasted_iota(jnp.int32, (1, LANES), 1)
    lane = lax.broadcasted_iota(jnp.int32, (n, LANES), 1)
    for sl in range(C_W // LANES):
        cols = slice(sl * LANES, (sl + 1) * LANES)
        s = _dot_nt(_pair_lhs(q_ref[:, cols], lane_row), k_ref[:, cols])
        p = jnp.exp(s - jnp.max(s, axis=-1, keepdims=True))
        l = jnp.sum(p, axis=-1, keepdims=True)
        o = _dot(p.astype(BF16), v_ref[:, cols]) / l
        o_ref[:, cols] = jnp.where(lane < HALF, o[:n], o[n:]).astype(BF16)


def _mha_small(q, k, v):
    return pl.pallas_call(
        _mha_kernel,
        grid=(1,),
        in_specs=[_full(q.shape), _full(k.shape), _full(v.shape)],
        out_specs=_full(q.shape),
        out_shape=jax.ShapeDtypeStruct(q.shape, BF16),
        compiler_params=_cparams(1, 32),
        name="mha_small",
    )(q, k, v)


def _conv_kernel(y_ref, yp_ref, yn_ref, dw_ref, dwb_ref, lng_ref, lnb_ref, o_ref, ext, shifted, *, ts):
    i = pl.program_id(0)
    nb = pl.num_programs(0)
    ext[0:CONV_HALO] = jnp.where(i > 0, yp_ref[...], 0.0)
    ext[CONV_HALO:CONV_HALO + ts] = y_ref[...]
    ext[CONV_HALO + ts:] = jnp.where(i < nb - 1, yn_ref[...], 0.0)
    base = CONV_HALO - CONV_W // 2
    span = ts + (base + CONV_W - 1) // SUBLANES * SUBLANES
    for s in range(SUBLANES):
        shifted[s] = ext[s:s + span, :]
    acc = jnp.zeros((ts, D_CH), F32)
    for j in range(CONV_W):
        s, start = (base + j) % SUBLANES, (base + j) // SUBLANES * SUBLANES
        acc = acc + shifted[s, start:start + ts, :] * dw_ref[j:j + 1, :]
    yb = acc + dwb_ref[...]
    mu = jnp.mean(yb, axis=-1, keepdims=True)
    var = jnp.mean(jnp.square(yb - mu), axis=-1, keepdims=True)
    z = (yb - mu) * lax.rsqrt(var + EPS) * lng_ref[...] + lnb_ref[...]
    o_ref[...] = (z * jax.nn.sigmoid(z)).astype(BF16)


def _conv_module(y, dw, dwb, lng, lnb, ts):
    n = y.shape[0]
    hb = ts // CONV_HALO
    nh = n // CONV_HALO
    vec = pl.BlockSpec((1, D_CH), lambda i: (0, 0))
    return pl.pallas_call(
        functools.partial(_conv_kernel, ts=ts),
        grid=(n // ts,),
        in_specs=[pl.BlockSpec((ts, D_CH), lambda i: (i, 0)),
                  pl.BlockSpec((CONV_HALO, D_CH), lambda i: (jnp.maximum(i * hb - 1, 0), 0)),
                  pl.BlockSpec((CONV_HALO, D_CH), lambda i: (jnp.minimum((i + 1) * hb, nh - 1), 0)),
                  _full((CONV_W + 1, D_CH)), vec, vec, vec],
        out_specs=pl.BlockSpec((ts, D_CH), lambda i: (i, 0)),
        out_shape=jax.ShapeDtypeStruct((n, D_CH), BF16),
        scratch_shapes=[pltpu.VMEM((ts + 2 * CONV_HALO, D_CH), F32),
                        pltpu.VMEM((SUBLANES, ts + 2 * CONV_HALO - SUBLANES, D_CH), F32)],
        compiler_params=_cparams(1, 48),
        name="conv_module",
    )(y, y, y, dw, dwb, lng, lnb)


def _first_max2(vals):
    def first_max(vs):
        m = functools.reduce(jnp.maximum, vs)
        idx = len(vs) - 1
        for k in range(len(vs) - 2, -1, -1):
            idx = jnp.where(vs[k] == m, k, idx)
        return m, idx

    m1, i1 = first_max(vals)
    m2, i2 = first_max([jnp.where(i1 == k, -jnp.inf, v) for k, v in enumerate(vals)])
    return m1, m2, i1, i2


def _pick(idx, vals):
    out = vals[-1]
    for k in range(len(vals) - 2, -1, -1):
        out = jnp.where(idx == k, vals[k], out)
    return out


def _route_t(lt, rbc, tri, carry, ts):
    nk = EXPERTS_PER_GROUP
    aff = jax.nn.sigmoid(lt[:nk * SUBLANES])
    sel = aff + rbc[:nk * SUBLANES]
    sel_k = [sel[k * SUBLANES:(k + 1) * SUBLANES] for k in range(nk)]
    aff_k = [aff[k * SUBLANES:(k + 1) * SUBLANES] for k in range(nk)]
    m1, m2, _, _ = _first_max2(sel_k)
    gscore = m1 + m2
    best, bg = gscore[0:1], jnp.zeros((1, ts), jnp.int32)
    for g in range(1, N_GROUPS):
        upd = gscore[g:g + 1] > best
        best = jnp.where(upd, gscore[g:g + 1], best)
        bg = jnp.where(upd, g, bg)
    sel_b = [_pick(bg, [v[g:g + 1] for g in range(N_GROUPS)]) for v in sel_k]
    aff_b = [_pick(bg, [v[g:g + 1] for g in range(N_GROUPS)]) for v in aff_k]
    _, _, i1, i2 = _first_max2(sel_b)
    a1, a2 = _pick(i1, aff_b), _pick(i2, aff_b)
    e1 = bg * nk + i1
    e2 = bg * nk + i2
    erow = lax.broadcasted_iota(jnp.int32, (N_EXPERTS, ts), 0)
    hit1 = erow == e1
    hit2 = erow == e2
    onehot = jnp.where(hit1 | hit2, 1.0, 0.0)
    before = _dot(onehot.astype(BF16), tri) + carry[:, 0:1]
    r1 = jnp.sum(jnp.where(hit1, before, 0.0), axis=0, keepdims=True)
    r2 = jnp.sum(jnp.where(hit2, before, 0.0), axis=0, keepdims=True)
    new_carry = carry + jnp.sum(onehot, axis=1, keepdims=True)
    tot = a1 + a2
    rows = [e1.astype(F32), e2.astype(F32), a1 / tot, a2 / tot, r1, r2, jnp.zeros((2, ts), F32)]
    return jnp.concatenate(rows, axis=0), new_carry


def _post_mix_kernel(x_ref, a_ref, b_ref, w_ref, g1_ref, ng_ref, sc_ref, sh_ref, rwh_ref, rwl_ref, rb_ref, tri_ref,
                     cnt0_ref, xo_ref, h2_ref, route_ref, cnt_ref, *, ts):
    @pl.when(pl.program_id(0) == 0)
    def _():
        cnt_ref[...] = cnt0_ref[...]

    half = w_ref.shape[0] // 2
    mo = _dot(a_ref[...], w_ref[:half]) + _dot(b_ref[...], w_ref[half:])
    xn = x_ref[...] + g1_ref[...] * mo
    xo_ref[...] = xn
    h2 = _rms_mod(xn, ng_ref[...], sc_ref[...], sh_ref[...])
    _store_row_tiles(h2_ref, h2, ts)
    hh, hl = _split_bf16(h2)
    lt = _dot_nt(rwh_ref[...], hh) + _dot_nt(rwh_ref[...], hl) + _dot_nt(rwl_ref[...], hh)
    rt, cnt = _route_t(lt, rb_ref[...], tri_ref[...], cnt_ref[...], ts)
    cnt_ref[...] = cnt
    route_ref[...] = jnp.concatenate([rt, jnp.zeros((LANES - SUBLANES, ts), F32)], axis=0).T


def _post_mix(x, a, b, w_out, mods, router, cnt0, ts):
    n = x.shape[0]
    g1, ng, sc, sh = mods
    rwh, rwl, rbc = router
    tri = jnp.asarray(np.triu(np.ones((ts, ts), np.float32), 1), dtype=BF16)
    row = lambda i: (i, 0)
    vec = lambda w: pl.BlockSpec((1, w), lambda i: (0, 0))
    return pl.pallas_call(
        functools.partial(_post_mix_kernel, ts=ts),
        grid=(n // ts,),
        in_specs=[pl.BlockSpec((ts, D_MODEL), row), pl.BlockSpec((ts, a.shape[1]), row),
                  pl.BlockSpec((ts, b.shape[1]), row), _full(w_out.shape),
                  vec(D_MODEL), vec(D_MODEL), vec(D_MODEL), vec(D_MODEL),
                  _full(rwh.shape), _full(rwl.shape), _full(rbc.shape), _full(tri.shape),
                  _full((N_EXPERTS, LANES))],
        out_specs=[pl.BlockSpec((ts, D_MODEL), row), pl.BlockSpec((ts * SUBLANES, LANES), row),
                   pl.BlockSpec((ts, LANES), row), _full((N_EXPERTS, LANES))],
        out_shape=[jax.ShapeDtypeStruct((n, D_MODEL), F32), jax.ShapeDtypeStruct((n * SUBLANES, LANES), F32),
                   jax.ShapeDtypeStruct((n, LANES), F32), jax.ShapeDtypeStruct((N_EXPERTS, LANES), F32)],
        compiler_params=_cparams(1, 56),
        name="post_mix",
    )(x, a, b, w_out, g1, ng, sc, sh, rwh, rwl, rbc, tri, cnt0)


def _tile_copies(make_copy, ts):
    def issue(i, carry):
        for u in range(DMA_UNROLL):
            make_copy(i * DMA_UNROLL + u, 0).start(priority=0)
            make_copy(i * DMA_UNROLL + u, 1).start(priority=1)
        return carry

    def drain(i, carry):
        for u in range(DMA_UNROLL):
            make_copy(i * DMA_UNROLL + u, 0).wait()
            make_copy(i * DMA_UNROLL + u, 1).wait()
        return carry

    lax.fori_loop(0, ts // DMA_UNROLL, issue, 0)
    lax.fori_loop(0, ts // DMA_UNROLL, drain, 0)


def _dispatch_kernel(pend_ref, dest_ref, *rest, ts, tile_offsets):
    n_streams = len(tile_offsets) - 1
    h_refs = rest[:n_streams]
    buf_ref, zeros_scr, zsem, sem = rest[n_streams:]
    i = pl.program_id(0)
    pl.when(i == 0)(functools.partial(_define_padding, pend_ref, buf_ref, zeros_scr, zsem))
    for s, h_ref in enumerate(h_refs):

        def row_copy(r, k, h_ref=h_ref):
            src = h_ref.at[pl.ds(pl.multiple_of(r * SUBLANES, SUBLANES), SUBLANES)]
            dst = buf_ref.at[pl.ds(pl.multiple_of(dest_ref[0, 0, 2 * r + k], SUBLANES), SUBLANES)]
            return pltpu.make_async_copy(src, dst, sem)

        pl.when((i >= tile_offsets[s]) & (i < tile_offsets[s + 1]))(functools.partial(_tile_copies, row_copy, ts))


def _define_padding(pend_ref, buf_ref, zeros_scr, zsem):
    zeros_scr[...] = jnp.zeros_like(zeros_scr)
    block_rows = MOE_BLOCK * SUBLANES

    def zero_block(first_row):
        return pltpu.make_async_copy(zeros_scr, buf_ref.at[pl.ds(pl.multiple_of(first_row, SUBLANES), block_rows)], zsem)

    def last_block(e):
        return zero_block((pend_ref[e] - MOE_BLOCK) * SUBLANES)

    def nonempty(e):
        return pend_ref[e] > (pend_ref[e - 1] if e > 0 else 0)

    for e in range(N_EXPERTS):
        pl.when(nonempty(e))(lambda e=e: last_block(e).start())
    for e in range(N_EXPERTS):
        pl.when(nonempty(e))(lambda e=e: last_block(e).wait())
    n_used = pend_ref[N_EXPERTS - 1] // MOE_BLOCK
    n_blocks = buf_ref.shape[0] // block_rows
    lax.fori_loop(n_used, n_blocks, lambda b, c: (zero_block(b * block_rows).start(), c)[1], 0)
    lax.fori_loop(n_used, n_blocks, lambda b, c: (zero_block(b * block_rows).wait(), c)[1], 0)


def _dispatch(h2ts, dest, pend, n_rows, ts):
    tiles = [h.shape[0] // (SUBLANES * ts) for h in h2ts]
    offs = [sum(tiles[:s]) for s in range(len(tiles) + 1)]
    in_specs = [pl.BlockSpec((1, 1, 2 * ts), lambda i, pe: (i, 0, 0), memory_space=pltpu.SMEM)]
    for s in range(len(h2ts)):
        in_specs.append(pl.BlockSpec((ts * SUBLANES, LANES),
                                     lambda i, pe, s=s: (jnp.clip(i - offs[s], 0, tiles[s] - 1), 0)))
    grid_spec = pltpu.PrefetchScalarGridSpec(
        num_scalar_prefetch=1, grid=(offs[-1],), in_specs=in_specs,
        out_specs=pl.BlockSpec(memory_space=pl.ANY),
        scratch_shapes=[pltpu.VMEM((MOE_BLOCK * SUBLANES, LANES), F32), pltpu.SemaphoreType.DMA(()),
                        pltpu.SemaphoreType.DMA(())])
    return pl.pallas_call(
        functools.partial(_dispatch_kernel, ts=ts, tile_offsets=tuple(offs)),
        grid_spec=grid_spec,
        out_shape=jax.ShapeDtypeStruct((n_rows * SUBLANES, LANES), F32),
        compiler_params=_cparams(1, 32),
        name="moe_dispatch",
    )(pend, dest.reshape(offs[-1], 1, 2 * ts), *h2ts)


def _expert_kernel(blk_e_ref, n_used_ref, x_ref, wg_ref, wu_ref, wd_ref, y_ref, wg_s, wu_s, wd_s):
    i = pl.program_id(0)

    @pl.when((i == 0) | (blk_e_ref[i] != blk_e_ref[jnp.maximum(i - 1, 0)]))
    def _():
        wg_s[...] = wg_ref[0, 0].astype(BF16)
        wu_s[...] = wu_ref[0, 0].astype(BF16)
        wd_s[...] = wd_ref[0, 0].astype(BF16)

    @pl.when(i < n_used_ref[0])
    def _():
        xb = _load_row_tiles(x_ref, MOE_BLOCK).astype(BF16)
        gate = _dot(xb, wg_s[...])
        up = _dot(xb, wu_s[...])
        hmid = gate * jax.nn.sigmoid(gate) * up
        _store_row_tiles(y_ref, _dot(hmid.astype(BF16), wd_s[...]), MOE_BLOCK)

    @pl.when(i >= n_used_ref[0])
    def _():
        y_ref[...] = jnp.zeros_like(y_ref)


def _experts(buf, blk_e, n_used, layer, wg, wu, wd):
    nb = buf.shape[0] // (MOE_BLOCK * SUBLANES)
    used = lambda i, be, nu: (jnp.minimum(i, nu[0] - 1), 0)
    wsel = lambda i, be, nu: (layer, be[i], 0, 0)
    grid_spec = pltpu.PrefetchScalarGridSpec(
        num_scalar_prefetch=2,
        grid=(nb,),
        in_specs=[pl.BlockSpec((MOE_BLOCK * SUBLANES, LANES), used),
                  pl.BlockSpec((1, 1, D_MODEL, D_EXPERT), wsel),
                  pl.BlockSpec((1, 1, D_MODEL, D_EXPERT), wsel),
                  pl.BlockSpec((1, 1, D_EXPERT, D_MODEL), wsel)],
        out_specs=pl.BlockSpec((MOE_BLOCK * SUBLANES, LANES), lambda i, be, nu: (i, 0)),
        scratch_shapes=[pltpu.VMEM((D_MODEL, D_EXPERT), BF16), pltpu.VMEM((D_MODEL, D_EXPERT), BF16),
                        pltpu.VMEM((D_EXPERT, D_MODEL), BF16)],
    )
    return pl.pallas_call(
        _expert_kernel,
        grid_spec=grid_spec,
        out_shape=jax.ShapeDtypeStruct(buf.shape, F32),
        compiler_params=_cparams(1, 48),
        name="moe_experts",
    )(blk_e, n_used, buf, wg, wu, wd)


def _combine_kernel(dest_ref, dest_next_ref, x_ref, route_ref, g2_ref, y_ref, o_ref, gbuf, sems, *, ts):
    i = pl.program_id(0)
    nt = pl.num_programs(0)

    def gather(d_ref, slot):
        def row_copy(r, k):
            src = y_ref.at[pl.ds(pl.multiple_of(d_ref[0, 0, 2 * r + k], SUBLANES), SUBLANES)]
            dst = gbuf.at[slot, k, pl.ds(pl.multiple_of(r * SUBLANES, SUBLANES), SUBLANES)]
            return pltpu.make_async_copy(src, dst, sems.at[slot])
        return row_copy

    def start_all(make_copy):
        def issue(b, carry):
            for u in range(DMA_UNROLL):
                make_copy(b * DMA_UNROLL + u, 0).start(priority=0)
                make_copy(b * DMA_UNROLL + u, 1).start(priority=1)
            return carry
        lax.fori_loop(0, ts // DMA_UNROLL, issue, 0)

    def wait_all(make_copy):
        def drain(b, carry):
            for u in range(DMA_UNROLL):
                make_copy(b * DMA_UNROLL + u, 0).wait()
                make_copy(b * DMA_UNROLL + u, 1).wait()
            return carry
        lax.fori_loop(0, ts // DMA_UNROLL, drain, 0)

    slot = i % 2
    pl.when(i == 0)(lambda: start_all(gather(dest_ref, 0)))
    pl.when(i + 1 < nt)(lambda: start_all(gather(dest_next_ref, 1 - slot)))
    wait_all(gather(dest_ref, slot))
    route = route_ref[...]
    w0, w1 = route[:, 2:3], route[:, 3:4]
    for k in range(SUBLANES):
        cols = slice(k * LANES, (k + 1) * LANES)
        f = (gbuf[slot, 0, pl.ds(k, ts, stride=SUBLANES), :] * w0
             + gbuf[slot, 1, pl.ds(k, ts, stride=SUBLANES), :] * w1)
        o_ref[:, cols] = x_ref[:, cols] + g2_ref[:, cols] * f


def _combine(x, route, g2, y, dest3, ts):
    n = x.shape[0]
    nt = n // ts
    row = lambda i: (i, 0)
    dest_spec = lambda f: pl.BlockSpec((1, 1, 2 * ts), f, memory_space=pltpu.SMEM)
    return pl.pallas_call(
        functools.partial(_combine_kernel, ts=ts),
        grid=(nt,),
        in_specs=[dest_spec(lambda i: (i, 0, 0)), dest_spec(lambda i: (jnp.minimum(i + 1, nt - 1), 0, 0)),
                  pl.BlockSpec((ts, D_MODEL), row), pl.BlockSpec((ts, LANES), row),
                  pl.BlockSpec((1, D_MODEL), lambda i: (0, 0)),
                  pl.BlockSpec(memory_space=pl.ANY)],
        out_specs=pl.BlockSpec((ts, D_MODEL), row),
        out_shape=jax.ShapeDtypeStruct((n, D_MODEL), F32),
        scratch_shapes=[pltpu.VMEM((2, 2, ts * SUBLANES, LANES), F32), pltpu.SemaphoreType.DMA((2,))],
        compiler_params=_cparams(1, 48),
        name="moe_combine",
    )(dest3, dest3, x, route, g2, y)


def _moe(streams, counts, layer, wg, wu, wd):
    experts = jnp.arange(N_EXPERTS, dtype=jnp.int32)
    cnt = counts[:, 0].astype(jnp.int32)
    pcounts = (cnt + MOE_BLOCK - 1) // MOE_BLOCK * MOE_BLOCK
    pend = jnp.cumsum(pcounts)
    pstart = pend - pcounts
    n_tok = sum(s[0].shape[0] for s in streams)
    n_blocks = -(-2 * n_tok // MOE_BLOCK) + N_EXPERTS
    blk_pos = jnp.arange(n_blocks, dtype=jnp.int32) * MOE_BLOCK
    blk_e = jnp.sum((pend[None, :] <= blk_pos[:, None]).astype(jnp.int32), axis=1)
    used = blk_pos < pend[-1]
    blk_e = jnp.where(used, blk_e, jnp.max(jnp.where(used, blk_e, 0)))
    n_used = (pend[-1:] // MOE_BLOCK).astype(jnp.int32)
    dests = []
    for x, h2, route, g2, ts in streams:
        e = route[:, 0:2].astype(jnp.int32)
        rank = route[:, 4:6].astype(jnp.int32)
        slot = jnp.sum(jnp.where(e[:, :, None] == experts, pstart, 0), axis=-1) + rank
        dests.append(slot * SUBLANES)
    ts_d = min(s[4] for s in streams)
    buf = _dispatch([s[1] for s in streams], jnp.concatenate(dests, axis=0), pend.astype(jnp.int32),
                    n_blocks * MOE_BLOCK, ts_d)
    y = _experts(buf, blk_e, n_used, layer, wg, wu, wd)
    return [_combine(x, route, g2, y, dest.reshape(x.shape[0] // ts, 1, 2 * ts), ts)
            for (x, _, route, g2, ts), dest in zip(streams, dests)]


def _block_diag_ones(width):
    idx = np.arange(width) // HALF
    return jnp.asarray((idx[:, None] == idx[None, :]).astype(np.float32), dtype=BF16)


def _rope_tables(n_tok):
    t = jnp.arange(n_tok, dtype=jnp.int32)
    pos = jnp.stack([t // GRID_W, t % GRID_W], axis=-1).astype(F32)
    inv = ROPE_THETA ** (-jnp.arange(ROPE_FREQS, dtype=F32) / ROPE_FREQS)
    ang = pos[:, :, None] * inv
    cos, sin = jnp.cos(ang), jnp.sin(ang)
    c = jnp.concatenate([cos, cos], axis=-1).reshape(n_tok, HEAD_DIM)
    s = jnp.concatenate([-sin, sin], axis=-1).reshape(n_tok, HEAD_DIM)
    return jnp.tile(c, (1, 2)), jnp.tile(s, (1, 2))


def _tile_vec(v, reps):
    return jnp.tile(v.astype(F32), reps).reshape(1, -1)


def kernel(x, c, ctx, c_ctx, router_w, router_b, ada_w, ada_b, norm1_g, norm2_g, ev_w_in, ev_w_out, a_q_norm,
           a_k_norm, b_v_norm, b_ws, b_bs, od_w_in, od_w_out, c_q_norm, c_k_norm, c_rpb, d_dw, d_dw_b, d_ln_g,
           d_ln_b, moe_w_gate, moe_w_up, moe_w_down):
    assert x.shape[0] == 1 and ctx.shape[0] == 1
    s_len, n_ctx = x.shape[1], ctx.shape[1]
    ts_lat = min(512, s_len)
    ts_ctx = n_ctx
    x_lat = x[0]
    x_ctx = ctx[0]

    cc = jnp.zeros((8, D_MODEL), F32).at[0].set(c[0]).at[1].set(c_ctx)
    mods = _modulation(cc, ada_w, ada_b)

    def mod(l, row, k):
        return mods[l, row, k * D_MODEL:(k + 1) * D_MODEL].reshape(1, D_MODEL)

    bd512 = _block_diag_ones(A_Q)
    bd128 = _block_diag_ones(LANES)
    cos_lat, sin_lat = _rope_tables(s_len)
    cos_ctx, sin_ctx = jnp.ones((n_ctx, LANES), F32), jnp.zeros((n_ctx, LANES), F32)
    e_of_row = np.arange(N_EXPERTS)
    row_of_e = (e_of_row % EXPERTS_PER_GROUP) * SUBLANES + e_of_row // EXPERTS_PER_GROUP
    rw = jnp.zeros((LANES, D_MODEL), F32).at[row_of_e].set(router_w.T)
    rwh = rw.astype(BF16)
    rwl = (rw - rwh.astype(F32)).astype(BF16)
    rbc = jnp.zeros((LANES, 1), F32).at[row_of_e, 0].set(router_b)
    router = (rwh, rwl, rbc)

    for l in range(DEPTH):
        need_ctx = l < DEPTH - 1
        i = l // 2
        g_n1 = norm1_g[l].reshape(1, D_MODEL)
        g_n2 = norm2_g[l].reshape(1, D_MODEL)
        m_lat = [mod(l, 0, k) for k in range(6)]
        m_ctx = [mod(l, 1, k) for k in range(6)]
        if l % 2 == 0:
            w_in = ev_w_in[i].astype(BF16)
            w_out = ev_w_out[i].astype(BF16)
            consts = (bd512, bd128, _tile_vec(a_q_norm[i], A_HEADS), _tile_vec(a_k_norm[i], A_KV_HEADS),
                      b_v_norm[i].reshape(1, B_WIDTH).astype(F32), b_ws[i].astype(BF16),
                      jnp.repeat(b_bs[i].T, HALF, axis=1))
            q_l, kd_l, v_l, b_l = _proj_even(x_lat, (g_n1, m_lat[1], m_lat[0]), w_in, consts, cos_lat, sin_lat, ts_lat)
            q_c, kd_c, v_c, b_c = _proj_even(x_ctx, (g_n1, m_ctx[1], m_ctx[0]), w_in, consts, cos_ctx, sin_ctx, ts_ctx)
            kd_cc = jnp.concatenate([kd_c, kd_c], axis=2)
            v_cc = jnp.concatenate([v_c, jnp.zeros_like(v_c)], axis=1)
            kd_all = jnp.concatenate([kd_l, kd_cc], axis=2)
            v_all = jnp.concatenate([v_l, v_cc], axis=1)
            mix_l = (_attn_even(q_l, kd_all, v_all, min(ATTN_TQ, s_len), ATTN_TK), b_l)
            if need_ctx:
                mix_c = (_attn_even(q_c, kd_cc, v_cc, ATTN_TQ, ATTN_TK), b_c)
        else:
            w_in = od_w_in[i].astype(BF16)
            w_out = od_w_out[i].astype(BF16)
            consts = (bd512, _tile_vec(c_q_norm[i], C_HEADS), _tile_vec(c_k_norm[i], C_HEADS))
            q_l, k_l, v_l, y_l = _proj_odd(x_lat, (g_n1, m_lat[1], m_lat[0]), w_in, consts, ts_lat)
            q_c, k_c, v_c, y_c = _proj_odd(x_ctx, (g_n1, m_ctx[1], m_ctx[0]), w_in, consts, ts_ctx)
            dw = jnp.zeros((CONV_W + 1, D_CH), F32).at[:CONV_W].set(d_dw[i])
            conv_p = (dw, d_dw_b[i].reshape(1, D_CH), d_ln_g[i].reshape(1, D_CH), d_ln_b[i].reshape(1, D_CH))
            mix_l = (_na_attention(q_l, k_l, v_l, k_c, v_c, _na_bias(c_rpb[i])), _conv_module(y_l, *conv_p, ts_lat))
            if need_ctx:
                mix_c = (_mha_small(q_c, k_c, v_c), _conv_module(y_c, *conv_p, ts_ctx))
        xn, h2, route, cnt = _post_mix(x_lat, *mix_l, w_out, (m_lat[2], g_n2, m_lat[4], m_lat[3]), router,
                                       jnp.zeros((N_EXPERTS, LANES), F32), ts_lat)
        streams = [(xn, h2, route, m_lat[5], ts_lat)]
        if need_ctx:
            xn, h2, route, cnt = _post_mix(x_ctx, *mix_c, w_out, (m_ctx[2], g_n2, m_ctx[4], m_ctx[3]), router,
                                           cnt, ts_ctx)
            streams.append((xn, h2, route, m_ctx[5], ts_ctx))
        outs = _moe(streams, cnt, l, moe_w_gate, moe_w_up, moe_w_down)
        x_lat = outs[0]
        if need_ctx:
            x_ctx = outs[1]
    return x_lat[None]
```

```python
import functools

import numpy as np
import jax
import jax.numpy as jnp
from jax import lax
from jax.experimental import pallas as pl
from jax.experimental.pallas import tpu as pltpu

F32 = jnp.float32
BF16 = jnp.bfloat16

D_MODEL = 1024
DEPTH = 4
GRID_W = 64
HEAD_DIM = 64
EPS = 1e-6
A_HEADS = 8
A_KV_HEADS = 2
A_Q = A_HEADS * HEAD_DIM
A_KV = A_KV_HEADS * HEAD_DIM
A_IN = A_Q + 2 * A_KV
ROPE_THETA = 10000.0
ROPE_FREQS = HEAD_DIM // 4
B_GROUPS = 8
B_WIDTH = 512
CHUNK = 128
C_HEADS = 8
C_W = C_HEADS * HEAD_DIM
NA_ROWS = 8
NA_COLS = 16
D_CH = 512
CONV_W = 31
EVEN_IN = A_IN + 2 * B_WIDTH
ODD_IN = 3 * C_W + 2 * D_CH
N_EXPERTS = 16
N_GROUPS = 4
EXPERTS_PER_GROUP = 4
D_EXPERT = 512
MOE_BLOCK = 256

LANES = 128
SUBLANES = 8
HALF = 64
NEG_BIG = -1e30
CONV_HALO = 16
DMA_UNROLL = 8
ATTN_TQ = 512
ATTN_TK = 512
QK_SCALE_LOG2 =HEAD_DIM ** -0.5 * 1.4426950408889634


def _cparams(n_axes, vmem_mb):
    return pltpu.CompilerParams(dimension_semantics=("arbitrary",) * n_axes,
                                vmem_limit_bytes=vmem_mb << 20)


def _full(shape):
    nd = len(shape)
    return pl.BlockSpec(shape, lambda *_: (0,) * nd)


def _resident(shape):
    nd = len(shape)
    return pl.BlockSpec(shape, lambda *_: (0,) * nd, pipeline_mode=pl.Buffered(1))


def _dot(a, b):
    return jnp.dot(a, b, preferred_element_type=F32)


def _dot_nt(a, b):
    return lax.dot_general(a, b, (((1,), (1,)), ((), ())), preferred_element_type=F32)


def _split_bf16(x):
    hi = x.astype(BF16)
    lo = (x - hi.astype(F32)).astype(BF16)
    return hi, lo


def _seg_mean_sq(x, bd):
    hi, lo = _split_bf16(x * x)
    return (_dot(hi, bd) + _dot(lo, bd)) * (1.0 / HALF)


def _rms_mod(x, g, sc, sh):
    ms = jnp.mean(x * x, axis=-1, keepdims=True)
    return x * lax.rsqrt(ms + EPS) * g * (1.0 + sc) + sh


def _rope(x, c, sg, lane):
    sw = jnp.where((lane % 32) < 16, pltpu.roll(x, LANES - 16, 1), pltpu.roll(x, 16, 1))
    return x * c + sw * sg


def _pair_lhs(qs, lane_row):
    ma = jnp.where(lane_row < HALF, 1.0, 0.0).astype(BF16)
    mb = jnp.where(lane_row < HALF, 0.0, 1.0).astype(BF16)
    return jnp.concatenate([qs * ma, qs * mb], axis=0)


def _store_row_tiles(ref, val, t, row0=0):
    for k in range(SUBLANES):
        ref[pl.ds(row0 * SUBLANES + k, t, stride=SUBLANES), :] = val[:, k * LANES:(k + 1) * LANES]


def _load_row_tiles(ref, t, row0=0):
    return jnp.concatenate([ref[pl.ds(row0 * SUBLANES + k, t, stride=SUBLANES), :] for k in range(SUBLANES)], axis=1)


def _mod_kernel(cc_ref, w_ref, b_ref, o_ref):
    cc = cc_ref[...]
    s = cc * jax.nn.sigmoid(cc)
    sh, sl = _split_bf16(s)
    wh, wl = _split_bf16(w_ref[0])
    o_ref[0] = _dot(sh, wh) + _dot(sl, wh) + _dot(sh, wl) + b_ref[0]


def _modulation(cc, ada_w, ada_b):
    tn = 1024
    return pl.pallas_call(
        _mod_kernel,
        grid=(DEPTH, 6 * D_MODEL // tn),
        in_specs=[pl.BlockSpec((8, D_MODEL), lambda l, j: (0, 0)),
                  pl.BlockSpec((1, D_MODEL, tn), lambda l, j: (l, 0, j)),
                  pl.BlockSpec((1, 1, tn), lambda l, j: (l, 0, j))],
        out_specs=pl.BlockSpec((1, 8, tn), lambda l, j: (l, 0, j)),
        out_shape=jax.ShapeDtypeStruct((DEPTH, 8, 6 * D_MODEL), F32),
        compiler_params=_cparams(2, 32),
        name="modulation",
    )(cc, ada_w, ada_b.reshape(DEPTH, 1, 6 * D_MODEL))


def _proj_even_kernel(x_ref, g_ref, sc_ref, sh_ref, w_ref, bd512_ref, bd128_ref, qg_ref, kg_ref, vg_ref,
                      cos_ref, sin_ref, ws_ref, bsb_ref, q_ref, kd_ref, v_ref, b_ref, *, ts):
    h = _rms_mod(x_ref[...], g_ref[...], sc_ref[...], sh_ref[...])
    p = _dot(h.astype(BF16), w_ref[...])
    lane = lax.broadcasted_iota(jnp.int32, (ts, LANES), 1)
    c = cos_ref[...]
    sg = sin_ref[...]

    q = p[:, :A_Q]
    qn = q * lax.rsqrt(_seg_mean_sq(q, bd512_ref[...]) + EPS) * qg_ref[...]
    for sl in range(A_Q // LANES):
        xs = qn[:, sl * LANES:(sl + 1) * LANES]
        q_ref[:, sl * LANES:(sl + 1) * LANES] = (_rope(xs, c, sg, lane) * QK_SCALE_LOG2).astype(BF16)

    k = p[:, A_Q:A_Q + A_KV]
    kn = k * lax.rsqrt(_seg_mean_sq(k, bd128_ref[...]) + EPS) * kg_ref[...]
    kt = _rope(kn, c, sg, lane).T
    for j in range(A_KV_HEADS):
        kj = kt[j * HALF:(j + 1) * HALF]
        kd_ref[j] = jnp.concatenate([kj, kj], axis=0).astype(BF16)
    v = p[:, A_Q + A_KV:A_IN]
    v_ref[0] = jnp.where(lane < HALF, v, 1.0).astype(BF16)
    v_ref[1] = jnp.where(lane < HALF, pltpu.roll(v, HALF, 1), 1.0).astype(BF16)

    zu = jax.nn.gelu(p[:, A_IN:A_IN + B_WIDTH])
    zv = jax.nn.gelu(p[:, A_IN + B_WIDTH:])
    vn = (zv * lax.rsqrt(_seg_mean_sq(zv, bd512_ref[...]) + EPS) * vg_ref[...]).astype(BF16)
    lane_c = lax.broadcasted_iota(jnp.int32, (CHUNK, LANES), 1)
    for ch in range(ts // CHUNK):
        rows = slice(ch * CHUNK, (ch + 1) * CHUNK)
        for sl in range(B_WIDTH // LANES):
            cols = slice(sl * LANES, (sl + 1) * LANES)
            vs = vn[rows, cols]
            mixed = jnp.where(lane_c < HALF, _dot(ws_ref[2 * sl], vs), _dot(ws_ref[2 * sl + 1], vs))
            b_ref[rows, cols] = (zu[rows, cols] * (mixed + bsb_ref[:, cols])).astype(BF16)


def _proj_even(x, mods, w_in, consts, cos, sin, ts):
    n = x.shape[0]
    g, sc, sh = mods
    bd512, bd128, qg, kg, vg, ws, bsb = consts
    row = lambda i: (i, 0)
    vec = lambda w: pl.BlockSpec((1, w), lambda i: (0, 0))
    return pl.pallas_call(
        functools.partial(_proj_even_kernel, ts=ts),
        grid=(n // ts,),
        in_specs=[pl.BlockSpec((ts, D_MODEL), row), vec(D_MODEL), vec(D_MODEL), vec(D_MODEL),
                  _full((D_MODEL, EVEN_IN)), _full((A_Q, A_Q)), _full((LANES, LANES)),
                  vec(A_Q), vec(A_KV), vec(B_WIDTH),
                  pl.BlockSpec((ts, LANES), row), pl.BlockSpec((ts, LANES), row),
                  _full((B_GROUPS, CHUNK, CHUNK)), _full((CHUNK, B_WIDTH))],
        out_specs=[pl.BlockSpec((ts, A_Q), row),
                   pl.BlockSpec((A_KV_HEADS, LANES, ts), lambda i: (0, 0, i)),
                   pl.BlockSpec((A_KV_HEADS, ts, LANES), lambda i: (0, i, 0)),
                   pl.BlockSpec((ts, B_WIDTH), row)],
        out_shape=[jax.ShapeDtypeStruct((n, A_Q), BF16),
                   jax.ShapeDtypeStruct((A_KV_HEADS, LANES, n), BF16),
                   jax.ShapeDtypeStruct((A_KV_HEADS, n, LANES), BF16),
                   jax.ShapeDtypeStruct((n, B_WIDTH), BF16)],
        compiler_params=_cparams(1, 56),
        name="proj_even",
    )(x, g, sc, sh, w_in, bd512, bd128, qg, kg, vg, cos, sin, ws, bsb)


def _attn_even_kernel(q_ref, kd_ref, va_ref, o_ref, lhs_scr, s_a, s_b, mx_a, mx_b, m_scr, acc_scr,
                      *, tq, tk, n_chunks):
    rows = 4 * tq
    lane_row = lax.broadcasted_iota(jnp.int32, (1, LANES), 1)
    lane = lax.broadcasted_iota(jnp.int32, (tq, LANES), 1)
    for j in range(A_KV_HEADS):
        lhs_scr[...] = jnp.concatenate([_pair_lhs(q_ref[:, sl * LANES:(sl + 1) * LANES], lane_row)
                                        for sl in (2 * j, 2 * j + 1)], axis=0)
        m_scr[...] = jnp.full((rows, LANES), NEG_BIG, F32)
        acc_scr[...] = jnp.zeros((rows, LANES), F32)

        def scores(c, s_scr, mx_scr, j=j):
            off = pl.multiple_of(c * tk, tk)
            s = _dot(lhs_scr[...], kd_ref[j, :, pl.ds(off, tk)])
            s_scr[...] = s
            mx_scr[...] = jnp.broadcast_to(jnp.max(s, axis=-1, keepdims=True), (rows, LANES))

        def accumulate(c, s_scr, mx_scr, j=j):
            off = pl.multiple_of(c * tk, tk)
            m_old = m_scr[...]
            m_new = jnp.maximum(m_old, mx_scr[...])
            p = jnp.concatenate([jnp.exp2(s_scr[:, t * LANES:(t + 1) * LANES] - m_new)
                                 for t in range(tk // LANES)], axis=1).astype(BF16)
            pv = _dot(p, va_ref[j, pl.ds(off, tk), :])
            acc_scr[...] = jnp.exp2(m_old - m_new) * acc_scr[...] + pv
            m_scr[...] = m_new

        scores(0, s_a, mx_a)

        def body(i, carry):
            scores(2 * i + 1, s_b, mx_b)
            accumulate(2 * i, s_a, mx_a)
            scores(2 * i + 2, s_a, mx_a)
            accumulate(2 * i + 1, s_b, mx_b)
            return carry

        lax.fori_loop(0, (n_chunks - 1) // 2, body, 0, unroll=2)
        accumulate(n_chunks - 1, s_a, mx_a)
        acc = acc_scr[...]
        o = acc / pltpu.roll(acc, HALF, 1)
        for t, sl in enumerate((2 * j, 2 * j + 1)):
            oa = o[(2 * t) * tq:(2 * t + 1) * tq]
            ob = o[(2 * t + 1) * tq:(2 * t + 2) * tq]
            o_ref[:, sl * LANES:(sl + 1) * LANES] = jnp.where(lane < HALF, oa, pltpu.roll(ob, HALF, 1)).astype(BF16)


def _attn_even(q, kd, va, tq, tk):
    n = q.shape[0]
    n_chunks = kd.shape[2] // tk
    assert n_chunks % 2 == 1
    rows = 4 * tq
    big = lambda w: pltpu.VMEM((rows, w), F32)
    return pl.pallas_call(
        functools.partial(_attn_even_kernel, tq=tq, tk=tk, n_chunks=n_chunks),
        grid=(n // tq,),
        in_specs=[pl.BlockSpec((tq, A_Q), lambda i: (i, 0)), _resident(kd.shape), _resident(va.shape)],
        out_specs=pl.BlockSpec((tq, A_Q), lambda i: (i, 0)),
        out_shape=jax.ShapeDtypeStruct((n, A_Q), BF16),
        scratch_shapes=[pltpu.VMEM((rows, LANES), BF16), big(tk), big(tk), big(LANES), big(LANES),
                        big(LANES), big(LANES)],
        compiler_params=_cparams(1, 56),
        name="attn_even",
    )(q, kd, va)


def _proj_odd_kernel(x_ref, g_ref, sc_ref, sh_ref, w_ref, bd512_ref, qg_ref, kg_ref,
                     q_ref, k_ref, v_ref, y_ref):
    h = _rms_mod(x_ref[...], g_ref[...], sc_ref[...], sh_ref[...])
    p = _dot(h.astype(BF16), w_ref[...])
    bd = bd512_ref[...]
    q = p[:, :C_W]
    q_ref[...] = (q * lax.rsqrt(_seg_mean_sq(q, bd) + EPS) * qg_ref[...] * (HEAD_DIM ** -0.5)).astype(BF16)
    k = p[:, C_W:2 * C_W]
    k_ref[...] = (k * lax.rsqrt(_seg_mean_sq(k, bd) + EPS) * kg_ref[...]).astype(BF16)
    v_ref[...] = p[:, 2 * C_W:3 * C_W].astype(BF16)
    y_ref[...] = p[:, 3 * C_W:3 * C_W + D_CH] * jax.nn.sigmoid(p[:, 3 * C_W + D_CH:])


def _proj_odd(x, mods, w_in, consts, ts):
    n = x.shape[0]
    g, sc, sh = mods
    bd512, qg, kg = consts
    row = lambda i: (i, 0)
    vec = lambda w: pl.BlockSpec((1, w), lambda i: (0, 0))
    blk = pl.BlockSpec((ts, C_W), row)
    return pl.pallas_call(
        _proj_odd_kernel,
        grid=(n // ts,),
        in_specs=[pl.BlockSpec((ts, D_MODEL), row), vec(D_MODEL), vec(D_MODEL), vec(D_MODEL),
                  _full((D_MODEL, ODD_IN)), _full((C_W, C_W)), vec(C_W), vec(C_W)],
        out_specs=[blk, blk, blk, blk],
        out_shape=[jax.ShapeDtypeStruct((n, C_W), BF16), jax.ShapeDtypeStruct((n, C_W), BF16),
                   jax.ShapeDtypeStruct((n, C_W), BF16), jax.ShapeDtypeStruct((n, D_CH), F32)],
        compiler_params=_cparams(1, 56),
        name="proj_odd",
    )(x, g, sc, sh, w_in, bd512, qg, kg)


NA_BLOCK_ROWS = 8
NA_TOK = NA_BLOCK_ROWS * GRID_W
NA_WIN = NA_ROWS * GRID_W


def _na_kernel(q_ref, kp_ref, kc_ref, kn_ref, vp_ref, vc_ref, vn_ref, kx_ref, vx_ref, bias_ref, o_ref,
               kbuf, vbuf, *, rows):
    i = pl.program_id(0)
    for t, (kr, vr) in enumerate(((kp_ref, vp_ref), (kc_ref, vc_ref), (kn_ref, vn_ref))):
        kbuf[t * NA_TOK:(t + 1) * NA_TOK] = kr[...]
        vbuf[t * NA_TOK:(t + 1) * NA_TOK] = vr[...]
    lane_row = lax.broadcasted_iota(jnp.int32, (1, LANES), 1)
    lane = lax.broadcasted_iota(jnp.int32, (GRID_W, LANES), 1)
    for sl in range(C_W // LANES):
        cols = slice(sl * LANES, (sl + 1) * LANES)
        lhs = _pair_lhs(q_ref[:, cols], lane_row)
        s_ctx = _dot_nt(lhs, kx_ref[:, cols])
        sa, sb, starts = [], [], []
        for j in range(NA_BLOCK_ROWS):
            r = i * NA_BLOCK_ROWS + j
            r0 = jnp.clip(r - NA_ROWS // 2, 0, rows - NA_ROWS)
            start = pl.multiple_of((r0 - (i - 1) * NA_BLOCK_ROWS) * GRID_W, GRID_W)
            starts.append(start)
            qa = lhs[j * GRID_W:(j + 1) * GRID_W]
            qb = lhs[NA_TOK + j * GRID_W:NA_TOK + (j + 1) * GRID_W]
            kw = kbuf[pl.ds(start, NA_WIN), cols]
            s = _dot_nt(jnp.concatenate([qa, qb], axis=0), kw) + bias_ref[r0 - r + NA_ROWS - 1, sl]
            sa.append(s[:GRID_W])
            sb.append(s[GRID_W:])
        s_loc = jnp.concatenate(sa + sb, axis=0)
        m = jnp.maximum(jnp.max(s_loc, axis=-1, keepdims=True), jnp.max(s_ctx, axis=-1, keepdims=True))
        p_loc = jnp.exp(s_loc - m)
        p_ctx = jnp.exp(s_ctx - m)
        l = jnp.sum(p_loc, axis=-1, keepdims=True) + jnp.sum(p_ctx, axis=-1, keepdims=True)
        o_ctx = _dot(p_ctx.astype(BF16), vx_ref[:, cols])
        p_loc = p_loc.astype(BF16)
        for j in range(NA_BLOCK_ROWS):
            ra = slice(j * GRID_W, (j + 1) * GRID_W)
            rb = slice(NA_TOK + j * GRID_W, NA_TOK + (j + 1) * GRID_W)
            vw = vbuf[pl.ds(starts[j], NA_WIN), cols]
            o2 = _dot(jnp.concatenate([p_loc[ra], p_loc[rb]], axis=0), vw)
            oa = (o2[:GRID_W] + o_ctx[ra]) / l[ra]
            ob = (o2[GRID_W:] + o_ctx[rb]) / l[rb]
            o_ref[ra, cols] = jnp.where(lane < HALF, oa, ob).astype(BF16)


def _na_attention(q, k, v, kx, vx, bias):
    n = q.shape[0]
    nb = n // NA_TOK
    blk = lambda f: pl.BlockSpec((NA_TOK, C_W), f)
    prev = lambda i: (jnp.maximum(i - 1, 0), 0)
    cur = lambda i: (i, 0)
    nxt = lambda i: (jnp.minimum(i + 1, nb - 1), 0)
    return pl.pallas_call(
        functools.partial(_na_kernel, rows=n // GRID_W),
        grid=(nb,),
        in_specs=[blk(cur), blk(prev), blk(cur), blk(nxt), blk(prev), blk(cur), blk(nxt),
                  _full(kx.shape), _full(vx.shape), _full(bias.shape)],
        out_specs=blk(cur),
        out_shape=jax.ShapeDtypeStruct((n, C_W), BF16),
        scratch_shapes=[pltpu.VMEM((3 * NA_TOK, C_W), BF16), pltpu.VMEM((3 * NA_TOK, C_W), BF16)],
        compiler_params=_cparams(1, 56),
        name="na_attention",
    )(q, k, k, k, v, v, v, kx, vx, bias)


def _na_bias(rpb):
    cc = np.arange(GRID_W)
    c0 = np.clip(cc - NA_COLS // 2, 0, GRID_W - NA_COLS)
    kc = np.arange(GRID_W)
    inwin = (kc[None, :] >= c0[:, None]) & (kc[None, :] < c0[:, None] + NA_COLS)
    dc = np.clip(kc[None, :] - cc[:, None] + NA_COLS - 1, 0, 2 * NA_COLS - 2)
    n_dr, n_dc = 2 * NA_ROWS - 1, 2 * NA_COLS - 1
    sel = np.zeros((n_dc, GRID_W * GRID_W), np.float32)
    sel[dc.ravel(), np.arange(GRID_W * GRID_W)] = inwin.ravel()
    t2 = jnp.dot(rpb.astype(F32).reshape(C_HEADS * n_dr, n_dc), jnp.asarray(sel), precision=lax.Precision.HIGHEST)
    t2 = t2.reshape(C_HEADS, n_dr, GRID_W, GRID_W) + jnp.asarray(np.where(inwin, 0.0, NEG_BIG), F32)
    t = jnp.stack([t2[:, oi:oi + NA_ROWS] for oi in range(NA_ROWS)], axis=0)
    t = jnp.transpose(t, (0, 1, 3, 2, 4))
    return t.reshape(NA_ROWS, C_HEADS // 2, 2 * GRID_W, NA_WIN)


def _mha_kernel(q_ref, k_ref, v_ref, o_ref):
    n = q_ref.shape[0]
    lane_row = lax.broadcasted_iota(jnp.int32, (1, LANES), 1)
    lane = lax.broadcasted_iota(jnp.int32, (n, LANES), 1)
    for sl in range(C_W // LANES):
        cols = slice(sl * LANES, (sl + 1) * LANES)
        s = _dot_nt(_pair_lhs(q_ref[:, cols], lane_row), k_ref[:, cols])
        p = jnp.exp(s - jnp.max(s, axis=-1, keepdims=True))
        l = jnp.sum(p, axis=-1, keepdims=True)
        o = _dot(p.astype(BF16), v_ref[:, cols]) / l
        o_ref[:, cols] = jnp.where(lane < HALF, o[:n], o[n:]).astype(BF16)


def _mha_small(q, k, v):
    return pl.pallas_call(
        _mha_kernel,
        grid=(1,),
        in_specs=[_full(q.shape), _full(k.shape), _full(v.shape)],
        out_specs=_full(q.shape),
        out_shape=jax.ShapeDtypeStruct(q.shape, BF16),
        compiler_params=_cparams(1, 32),
        name="mha_small",
    )(q, k, v)


def _conv_kernel(y_ref, yp_ref, yn_ref, dw_ref, dwb_ref, lng_ref, lnb_ref, o_ref, ext, shifted, *, ts):
    i = pl.program_id(0)
    nb = pl.num_programs(0)
    ext[0:CONV_HALO] = jnp.where(i > 0, yp_ref[...], 0.0)
    ext[CONV_HALO:CONV_HALO + ts] = y_ref[...]
    ext[CONV_HALO + ts:] = jnp.where(i < nb - 1, yn_ref[...], 0.0)
    base = CONV_HALO - CONV_W // 2
    span = ts + (base + CONV_W - 1) // SUBLANES * SUBLANES
    for s in range(SUBLANES):
        shifted[s] = ext[s:s + span, :]
    acc = jnp.zeros((ts, D_CH), F32)
    for j in range(CONV_W):
        s, start = (base + j) % SUBLANES, (base + j) // SUBLANES * SUBLANES
        acc = acc + shifted[s, start:start + ts, :] * dw_ref[j:j + 1, :]
    yb = acc + dwb_ref[...]
    mu = jnp.mean(yb, axis=-1, keepdims=True)
    var = jnp.mean(jnp.square(yb - mu), axis=-1, keepdims=True)
    z = (yb - mu) * lax.rsqrt(var + EPS) * lng_ref[...] + lnb_ref[...]
    o_ref[...] = (z * jax.nn.sigmoid(z)).astype(BF16)


def _conv_module(y, dw, dwb, lng, lnb, ts):
    n = y.shape[0]
    hb = ts // CONV_HALO
    nh = n // CONV_HALO
    vec = pl.BlockSpec((1, D_CH), lambda i: (0, 0))
    return pl.pallas_call(
        functools.partial(_conv_kernel, ts=ts),
        grid=(n // ts,),
        in_specs=[pl.BlockSpec((ts, D_CH), lambda i: (i, 0)),
                  pl.BlockSpec((CONV_HALO, D_CH), lambda i: (jnp.maximum(i * hb - 1, 0), 0)),
                  pl.BlockSpec((CONV_HALO, D_CH), lambda i: (jnp.minimum((i + 1) * hb, nh - 1), 0)),
                  _full((CONV_W + 1, D_CH)), vec, vec, vec],
        out_specs=pl.BlockSpec((ts, D_CH), lambda i: (i, 0)),
        out_shape=jax.ShapeDtypeStruct((n, D_CH), BF16),
        scratch_shapes=[pltpu.VMEM((ts + 2 * CONV_HALO, D_CH), F32),
                        pltpu.VMEM((SUBLANES, ts + 2 * CONV_HALO - SUBLANES, D_CH), F32)],
        compiler_params=_cparams(1, 48),
        name="conv_module",
    )(y, y, y, dw, dwb, lng, lnb)


def _first_max2(vals):
    def first_max(vs):
        m = functools.reduce(jnp.maximum, vs)
        idx = len(vs) - 1
        for k in range(len(vs) - 2, -1, -1):
            idx = jnp.where(vs[k] == m, k, idx)
        return m, idx

    m1, i1 = first_max(vals)
    m2, i2 = first_max([jnp.where(i1 == k, -jnp.inf, v) for k, v in enumerate(vals)])
    return m1, m2, i1, i2


def _pick(idx, vals):
    out = vals[-1]
    for k in range(len(vals) - 2, -1, -1):
        out = jnp.where(idx == k, vals[k], out)
    return out


def _route_t(lt, rbc, tri, carry, ts):
    nk = EXPERTS_PER_GROUP
    aff = jax.nn.sigmoid(lt[:nk * SUBLANES])
    sel = aff + rbc[:nk * SUBLANES]
    sel_k = [sel[k * SUBLANES:(k + 1) * SUBLANES] for k in range(nk)]
    aff_k = [aff[k * SUBLANES:(k + 1) * SUBLANES] for k in range(nk)]
    m1, m2, _, _ = _first_max2(sel_k)
    gscore = m1 + m2
    best, bg = gscore[0:1], jnp.zeros((1, ts), jnp.int32)
    for g in range(1, N_GROUPS):
        upd = gscore[g:g + 1] > best
        best = jnp.where(upd, gscore[g:g + 1], best)
        bg = jnp.where(upd, g, bg)
    sel_b = [_pick(bg, [v[g:g + 1] for g in range(N_GROUPS)]) for v in sel_k]
    aff_b = [_pick(bg, [v[g:g + 1] for g in range(N_GROUPS)]) for v in aff_k]
    _, _, i1, i2 = _first_max2(sel_b)
    a1, a2 = _pick(i1, aff_b), _pick(i2, aff_b)
    e1 = bg * nk + i1
    e2 = bg * nk + i2
    erow = lax.broadcasted_iota(jnp.int32, (N_EXPERTS, ts), 0)
    hit1 = erow == e1
    hit2 = erow == e2
    onehot = jnp.where(hit1 | hit2, 1.0, 0.0)
    before = _dot(onehot.astype(BF16), tri) + carry[:, 0:1]
    r1 = jnp.sum(jnp.where(hit1, before, 0.0), axis=0, keepdims=True)
    r2 = jnp.sum(jnp.where(hit2, before, 0.0), axis=0, keepdims=True)
    new_carry = carry + jnp.sum(onehot, axis=1, keepdims=True)
    tot = a1 + a2
    rows = [e1.astype(F32), e2.astype(F32), a1 / tot, a2 / tot, r1, r2, jnp.zeros((2, ts), F32)]
    return jnp.concatenate(rows, axis=0), new_carry


def _post_mix_kernel(x_ref, a_ref, b_ref, w_ref, g1_ref, ng_ref, sc_ref, sh_ref, rwh_ref, rwl_ref, rb_ref, tri_ref,
                     cnt0_ref, xo_ref, h2_ref, route_ref, cnt_ref, *, ts):
    @pl.when(pl.program_id(0) == 0)
    def _():
        cnt_ref[...] = cnt0_ref[...]

    half = w_ref.shape[0] // 2
    mo = _dot(a_ref[...], w_ref[:half]) + _dot(b_ref[...], w_ref[half:])
    xn = x_ref[...] + g1_ref[...] * mo
    xo_ref[...] = xn
    h2 = _rms_mod(xn, ng_ref[...], sc_ref[...], sh_ref[...])
    _store_row_tiles(h2_ref, h2, ts)
    hh, hl = _split_bf16(h2)
    lt = _dot_nt(rwh_ref[...], hh) + _dot_nt(rwh_ref[...], hl) + _dot_nt(rwl_ref[...], hh)
    rt, cnt = _route_t(lt, rb_ref[...], tri_ref[...], cnt_ref[...], ts)
    cnt_ref[...] = cnt
    route_ref[...] = jnp.concatenate([rt, jnp.zeros((LANES - SUBLANES, ts), F32)], axis=0).T


def _post_mix(x, a, b, w_out, mods, router, cnt0, ts):
    n = x.shape[0]
    g1, ng, sc, sh = mods
    rwh, rwl, rbc = router
    tri = jnp.asarray(np.triu(np.ones((ts, ts), np.float32), 1), dtype=BF16)
    row = lambda i: (i, 0)
    vec = lambda w: pl.BlockSpec((1, w), lambda i: (0, 0))
    return pl.pallas_call(
        functools.partial(_post_mix_kernel, ts=ts),
        grid=(n // ts,),
        in_specs=[pl.BlockSpec((ts, D_MODEL), row), pl.BlockSpec((ts, a.shape[1]), row),
                  pl.BlockSpec((ts, b.shape[1]), row), _full(w_out.shape),
                  vec(D_MODEL), vec(D_MODEL), vec(D_MODEL), vec(D_MODEL),
                  _full(rwh.shape), _full(rwl.shape), _full(rbc.shape), _full(tri.shape),
                  _full((N_EXPERTS, LANES))],
        out_specs=[pl.BlockSpec((ts, D_MODEL), row), pl.BlockSpec((ts * SUBLANES, LANES), row),
                   pl.BlockSpec((ts, LANES), row), _full((N_EXPERTS, LANES))],
        out_shape=[jax.ShapeDtypeStruct((n, D_MODEL), F32), jax.ShapeDtypeStruct((n * SUBLANES, LANES), F32),
                   jax.ShapeDtypeStruct((n, LANES), F32), jax.ShapeDtypeStruct((N_EXPERTS, LANES), F32)],
        compiler_params=_cparams(1, 56),
        name="post_mix",
    )(x, a, b, w_out, g1, ng, sc, sh, rwh, rwl, rbc, tri, cnt0)


def _tile_copies(make_copy, ts):
    def issue(i, carry):
        for u in range(DMA_UNROLL):
            make_copy(i * DMA_UNROLL + u, 0).start(priority=0)
            make_copy(i * DMA_UNROLL + u, 1).start(priority=1)
        return carry

    def drain(i, carry):
        for u in range(DMA_UNROLL):
            make_copy(i * DMA_UNROLL + u, 0).wait()
            make_copy(i * DMA_UNROLL + u, 1).wait()
        return carry

    lax.fori_loop(0, ts // DMA_UNROLL, issue, 0)
    lax.fori_loop(0, ts // DMA_UNROLL, drain, 0)


def _dispatch_kernel(pend_ref, dest_ref, *rest, tss, tile_offsets):
    n_streams = len(tile_offsets) - 1
    h_refs = rest[:n_streams]
    buf_ref, zeros_scr, zsem, sem = rest[n_streams:]
    i = pl.program_id(0)
    pl.when(i == 0)(functools.partial(_define_padding, pend_ref, buf_ref, zeros_scr, zsem))
    for s, h_ref in enumerate(h_refs):

        def row_copy(r, k, h_ref=h_ref):
            src = h_ref.at[pl.ds(pl.multiple_of(r * SUBLANES, SUBLANES), SUBLANES)]
            dst = buf_ref.at[pl.ds(pl.multiple_of(dest_ref[0, 0, 2 * r + k], SUBLANES), SUBLANES)]
            return pltpu.make_async_copy(src, dst, sem)

        pl.when((i >= tile_offsets[s]) & (i < tile_offsets[s + 1]))(
            functools.partial(_tile_copies, row_copy, tss[s]))


def _define_padding(pend_ref, buf_ref, zeros_scr, zsem):
    zeros_scr[...] = jnp.zeros_like(zeros_scr)
    block_rows = MOE_BLOCK * SUBLANES

    def zero_block(first_row):
        return pltpu.make_async_copy(zeros_scr, buf_ref.at[pl.ds(pl.multiple_of(first_row, SUBLANES), block_rows)], zsem)

    def last_block(e):
        return zero_block((pend_ref[e] - MOE_BLOCK) * SUBLANES)

    def nonempty(e):
        return pend_ref[e] > (pend_ref[e - 1] if e > 0 else 0)

    for e in range(N_EXPERTS):
        pl.when(nonempty(e))(lambda e=e: last_block(e).start())
    for e in range(N_EXPERTS):
        pl.when(nonempty(e))(lambda e=e: last_block(e).wait())
    n_used = pend_ref[N_EXPERTS - 1] // MOE_BLOCK
    n_blocks = buf_ref.shape[0] // block_rows
    lax.fori_loop(n_used, n_blocks, lambda b, c: (zero_block(b * block_rows).start(), c)[1], 0)
    lax.fori_loop(n_used, n_blocks, lambda b, c: (zero_block(b * block_rows).wait(), c)[1], 0)


def _dispatch(h2ts, dests, tss, pend, n_rows):
    tiles = [h.shape[0] // (SUBLANES * ts) for h, ts in zip(h2ts, tss)]
    offs = [sum(tiles[:s]) for s in range(len(tiles) + 1)]
    width = 2 * max(tss)
    dest3 = jnp.concatenate([jnp.pad(d.reshape(t, 2 * ts), ((0, 0), (0, width - 2 * ts)))
                             for d, t, ts in zip(dests, tiles, tss)], axis=0).reshape(offs[-1], 1, width)
    in_specs = [pl.BlockSpec((1, 1, width), lambda i, pe: (i, 0, 0), memory_space=pltpu.SMEM)]
    for s in range(len(h2ts)):
        in_specs.append(pl.BlockSpec((tss[s] * SUBLANES, LANES),
                                     lambda i, pe, s=s: (jnp.clip(i - offs[s], 0, tiles[s] - 1), 0)))
    grid_spec = pltpu.PrefetchScalarGridSpec(
        num_scalar_prefetch=1, grid=(offs[-1],), in_specs=in_specs,
        out_specs=pl.BlockSpec(memory_space=pl.ANY),
        scratch_shapes=[pltpu.VMEM((MOE_BLOCK * SUBLANES, LANES), F32), pltpu.SemaphoreType.DMA(()),
                        pltpu.SemaphoreType.DMA(())])
    return pl.pallas_call(
        functools.partial(_dispatch_kernel, tss=tuple(tss), tile_offsets=tuple(offs)),
        grid_spec=grid_spec,
        out_shape=jax.ShapeDtypeStruct((n_rows * SUBLANES, LANES), F32),
        compiler_params=_cparams(1, 32),
        name="moe_dispatch",
    )(pend, dest3, *h2ts)


def _expert_kernel(blk_e_ref, n_used_ref, x_ref, wg_ref, wu_ref, wd_ref, y_ref, wg_s, wu_s, wd_s):
    i = pl.program_id(0)

    @pl.when((i == 0) | (blk_e_ref[i] != blk_e_ref[jnp.maximum(i - 1, 0)]))
    def _():
        wg_s[...] = wg_ref[0, 0].astype(BF16)
        wu_s[...] = wu_ref[0, 0].astype(BF16)
        wd_s[...] = wd_ref[0, 0].astype(BF16)

    @pl.when(i < n_used_ref[0])
    def _():
        xb = _load_row_tiles(x_ref, MOE_BLOCK).astype(BF16)
        gate = _dot(xb, wg_s[...])
        up = _dot(xb, wu_s[...])
        hmid = gate * jax.nn.sigmoid(gate) * up
        _store_row_tiles(y_ref, _dot(hmid.astype(BF16), wd_s[...]), MOE_BLOCK)

    @pl.when(i >= n_used_ref[0])
    def _():
        y_ref[...] = jnp.zeros_like(y_ref)


def _experts(buf, blk_e, n_used, layer, wg, wu, wd):
    nb = buf.shape[0] // (MOE_BLOCK * SUBLANES)
    used = lambda i, be, nu: (jnp.minimum(i, nu[0] - 1), 0)
    wsel = lambda i, be, nu: (layer, be[i], 0, 0)
    grid_spec = pltpu.PrefetchScalarGridSpec(
        num_scalar_prefetch=2,
        grid=(nb,),
        in_specs=[pl.BlockSpec((MOE_BLOCK * SUBLANES, LANES), used),
                  pl.BlockSpec((1, 1, D_MODEL, D_EXPERT), wsel),
                  pl.BlockSpec((1, 1, D_MODEL, D_EXPERT), wsel),
                  pl.BlockSpec((1, 1, D_EXPERT, D_MODEL), wsel)],
        out_specs=pl.BlockSpec((MOE_BLOCK * SUBLANES, LANES), lambda i, be, nu: (i, 0)),
        scratch_shapes=[pltpu.VMEM((D_MODEL, D_EXPERT), BF16), pltpu.VMEM((D_MODEL, D_EXPERT), BF16),
                        pltpu.VMEM((D_EXPERT, D_MODEL), BF16)],
    )
    return pl.pallas_call(
        _expert_kernel,
        grid_spec=grid_spec,
        out_shape=jax.ShapeDtypeStruct(buf.shape, F32),
        compiler_params=_cparams(1, 48),
        name="moe_experts",
    )(blk_e, n_used, buf, wg, wu, wd)


def _combine_kernel(dest_ref, dest_next_ref, x_ref, route_ref, g2_ref, y_ref, o_ref, gbuf, sems, *, ts):
    i = pl.program_id(0)
    nt = pl.num_programs(0)

    def gather(d_ref, slot):
        def row_copy(r, k):
            src = y_ref.at[pl.ds(pl.multiple_of(d_ref[0, 0, 2 * r + k], SUBLANES), SUBLANES)]
            dst = gbuf.at[slot, k, pl.ds(pl.multiple_of(r * SUBLANES, SUBLANES), SUBLANES)]
            return pltpu.make_async_copy(src, dst, sems.at[slot])
        return row_copy

    def start_all(make_copy):
        def issue(b, carry):
            for u in range(DMA_UNROLL):
                make_copy(b * DMA_UNROLL + u, 0).start(priority=0)
                make_copy(b * DMA_UNROLL + u, 1).start(priority=1)
            return carry
        lax.fori_loop(0, ts // DMA_UNROLL, issue, 0)

    def wait_all(make_copy):
        def drain(b, carry):
            for u in range(DMA_UNROLL):
                make_copy(b * DMA_UNROLL + u, 0).wait()
                make_copy(b * DMA_UNROLL + u, 1).wait()
            return carry
        lax.fori_loop(0, ts // DMA_UNROLL, drain, 0)

    slot = i % 2
    pl.when(i == 0)(lambda: start_all(gather(dest_ref, 0)))
    pl.when(i + 1 < nt)(lambda: start_all(gather(dest_next_ref, 1 - slot)))
    wait_all(gather(dest_ref, slot))
    route = route_ref[...]
    w0, w1 = route[:, 2:3], route[:, 3:4]
    for k in range(SUBLANES):
        cols = slice(k * LANES, (k + 1) * LANES)
        f = (gbuf[slot, 0, pl.ds(k, ts, stride=SUBLANES), :] * w0
             + gbuf[slot, 1, pl.ds(k, ts, stride=SUBLANES), :] * w1)
        o_ref[:, cols] = x_ref[:, cols] + g2_ref[:, cols] * f


def _combine(x, route, g2, y, dest3, ts):
    n = x.shape[0]
    nt = n // ts
    row = lambda i: (i, 0)
    dest_spec = lambda f: pl.BlockSpec((1, 1, 2 * ts), f, memory_space=pltpu.SMEM)
    return pl.pallas_call(
        functools.partial(_combine_kernel, ts=ts),
        grid=(nt,),
        in_specs=[dest_spec(lambda i: (i, 0, 0)), dest_spec(lambda i: (jnp.minimum(i + 1, nt - 1), 0, 0)),
                  pl.BlockSpec((ts, D_MODEL), row), pl.BlockSpec((ts, LANES), row),
                  pl.BlockSpec((1, D_MODEL), lambda i: (0, 0)),
                  pl.BlockSpec(memory_space=pl.ANY)],
        out_specs=pl.BlockSpec((ts, D_MODEL), row),
        out_shape=jax.ShapeDtypeStruct((n, D_MODEL), F32),
        scratch_shapes=[pltpu.VMEM((2, 2, ts * SUBLANES, LANES), F32), pltpu.SemaphoreType.DMA((2,))],
        compiler_params=_cparams(1, 48),
        name="moe_combine",
    )(dest3, dest3, x, route, g2, y)


def _moe(streams, counts, layer, wg, wu, wd):
    experts = jnp.arange(N_EXPERTS, dtype=jnp.int32)
    cnt = counts[:, 0].astype(jnp.int32)
    pcounts = (cnt + MOE_BLOCK - 1) // MOE_BLOCK * MOE_BLOCK
    pend = jnp.cumsum(pcounts)
    pstart = pend - pcounts
    n_tok = sum(s[0].shape[0] for s in streams)
    n_blocks = -(-2 * n_tok // MOE_BLOCK) + N_EXPERTS
    blk_pos = jnp.arange(n_blocks, dtype=jnp.int32) * MOE_BLOCK
    blk_e = jnp.sum((pend[None, :] <= blk_pos[:, None]).astype(jnp.int32), axis=1)
    used = blk_pos < pend[-1]
    blk_e = jnp.where(used, blk_e, jnp.max(jnp.where(used, blk_e, 0)))
    n_used = (pend[-1:] // MOE_BLOCK).astype(jnp.int32)
    dests = []
    for x, h2, route, g2, ts in streams:
        e = route[:, 0:2].astype(jnp.int32)
        rank = route[:, 4:6].astype(jnp.int32)
        slot = jnp.sum(jnp.where(e[:, :, None] == experts, pstart, 0), axis=-1) + rank
        dests.append(slot * SUBLANES)
    buf = _dispatch([s[1] for s in streams], dests, [s[4] for s in streams], pend.astype(jnp.int32),
                    n_blocks * MOE_BLOCK)
    y = _experts(buf, blk_e, n_used, layer, wg, wu, wd)
    return [_combine(x, route, g2, y, dest.reshape(x.shape[0] // ts, 1, 2 * ts), ts)
            for (x, _, route, g2, ts), dest in zip(streams, dests)]


def _block_diag_ones(width):
    idx = np.arange(width) // HALF
    return jnp.asarray((idx[:, None] == idx[None, :]).astype(np.float32), dtype=BF16)


def _rope_tables(n_tok):
    t = jnp.arange(n_tok, dtype=jnp.int32)
    pos = jnp.stack([t // GRID_W, t % GRID_W], axis=-1).astype(F32)
    inv = ROPE_THETA ** (-jnp.arange(ROPE_FREQS, dtype=F32) / ROPE_FREQS)
    ang = pos[:, :, None] * inv
    cos, sin = jnp.cos(ang), jnp.sin(ang)
    c = jnp.concatenate([cos, cos], axis=-1).reshape(n_tok, HEAD_DIM)
    s = jnp.concatenate([-sin, sin], axis=-1).reshape(n_tok, HEAD_DIM)
    return jnp.tile(c, (1, 2)), jnp.tile(s, (1, 2))


def _tile_vec(v, reps):
    return jnp.tile(v.astype(F32), reps).reshape(1, -1)


def kernel(x, c, ctx, c_ctx, router_w, router_b, ada_w, ada_b, norm1_g, norm2_g, ev_w_in, ev_w_out, a_q_norm,
           a_k_norm, b_v_norm, b_ws, b_bs, od_w_in, od_w_out, c_q_norm, c_k_norm, c_rpb, d_dw, d_dw_b, d_ln_g,
           d_ln_b, moe_w_gate, moe_w_up, moe_w_down):
    assert x.shape[0] == 1 and ctx.shape[0] == 1
    s_len, n_ctx = x.shape[1], ctx.shape[1]
    ts_lat = min(512, s_len)
    ts_ctx = n_ctx
    x_lat = x[0]
    x_ctx = ctx[0]

    cc = jnp.zeros((8, D_MODEL), F32).at[0].set(c[0]).at[1].set(c_ctx)
    mods = _modulation(cc, ada_w, ada_b)

    def mod(l, row, k):
        return mods[l, row, k * D_MODEL:(k + 1) * D_MODEL].reshape(1, D_MODEL)

    bd512 = _block_diag_ones(A_Q)
    bd128 = _block_diag_ones(LANES)
    cos_lat, sin_lat = _rope_tables(s_len)
    cos_ctx, sin_ctx = jnp.ones((n_ctx, LANES), F32), jnp.zeros((n_ctx, LANES), F32)
    e_of_row = np.arange(N_EXPERTS)
    row_of_e = (e_of_row % EXPERTS_PER_GROUP) * SUBLANES + e_of_row // EXPERTS_PER_GROUP
    rw = jnp.zeros((LANES, D_MODEL), F32).at[row_of_e].set(router_w.T)
    rwh = rw.astype(BF16)
    rwl = (rw - rwh.astype(F32)).astype(BF16)
    rbc = jnp.zeros((LANES, 1), F32).at[row_of_e, 0].set(router_b)
    router = (rwh, rwl, rbc)

    for l in range(DEPTH):
        need_ctx = l < DEPTH - 1
        i = l // 2
        g_n1 = norm1_g[l].reshape(1, D_MODEL)
        g_n2 = norm2_g[l].reshape(1, D_MODEL)
        m_lat = [mod(l, 0, k) for k in range(6)]
        m_ctx = [mod(l, 1, k) for k in range(6)]
        if l % 2 == 0:
            w_in = ev_w_in[i].astype(BF16)
            w_out = ev_w_out[i].astype(BF16)
            consts = (bd512, bd128, _tile_vec(a_q_norm[i], A_HEADS), _tile_vec(a_k_norm[i], A_KV_HEADS),
                      b_v_norm[i].reshape(1, B_WIDTH).astype(F32), b_ws[i].astype(BF16),
                      jnp.repeat(b_bs[i].T, HALF, axis=1))
            q_l, kd_l, v_l, b_l = _proj_even(x_lat, (g_n1, m_lat[1], m_lat[0]), w_in, consts, cos_lat, sin_lat, ts_lat)
            q_c, kd_c, v_c, b_c = _proj_even(x_ctx, (g_n1, m_ctx[1], m_ctx[0]), w_in, consts, cos_ctx, sin_ctx, ts_ctx)
            kd_cc = jnp.concatenate([kd_c, kd_c], axis=2)
            v_cc = jnp.concatenate([v_c, jnp.zeros_like(v_c)], axis=1)
            kd_all = jnp.concatenate([kd_l, kd_cc], axis=2)
            v_all = jnp.concatenate([v_l, v_cc], axis=1)
            mix_l = (_attn_even(q_l, kd_all, v_all, min(ATTN_TQ, s_len), ATTN_TK), b_l)
            if need_ctx:
                mix_c = (_attn_even(q_c, kd_cc, v_cc, min(ATTN_TQ, n_ctx), ATTN_TK), b_c)
        else:
            w_in = od_w_in[i].astype(BF16)
            w_out = od_w_out[i].astype(BF16)
            consts = (bd512, _tile_vec(c_q_norm[i], C_HEADS), _tile_vec(c_k_norm[i], C_HEADS))
            q_l, k_l, v_l, y_l = _proj_odd(x_lat, (g_n1, m_lat[1], m_lat[0]), w_in, consts, ts_lat)
            q_c, k_c, v_c, y_c = _proj_odd(x_ctx, (g_n1, m_ctx[1], m_ctx[0]), w_in, consts, ts_ctx)
            dw = jnp.zeros((CONV_W + 1, D_CH), F32).at[:CONV_W].set(d_dw[i])
            conv_p = (dw, d_dw_b[i].reshape(1, D_CH), d_ln_g[i].reshape(1, D_CH), d_ln_b[i].reshape(1, D_CH))
            mix_l = (_na_attention(q_l, k_l, v_l, k_c, v_c, _na_bias(c_rpb[i])), _conv_module(y_l, *conv_p, ts_lat))
            if need_ctx:
                mix_c = (_mha_small(q_c, k_c, v_c), _conv_module(y_c, *conv_p, ts_ctx))
        xn, h2, route, cnt = _post_mix(x_lat, *mix_l, w_out, (m_lat[2], g_n2, m_lat[4], m_lat[3]), router,
                                       jnp.zeros((N_EXPERTS, LANES), F32), ts_lat)
        streams = [(xn, h2, route, m_lat[5], ts_lat)]
        if need_ctx:
            xn, h2, route, cnt = _post_mix(x_ctx, *mix_c, w_out, (m_ctx[2], g_n2, m_ctx[4], m_ctx[3]), router,
                                           cnt, ts_ctx)
            streams.append((xn, h2, route, m_ctx[5], ts_ctx))
        outs = _moe(streams, cnt, l, moe_w_gate, moe_w_up, moe_w_down)
        x_lat = outs[0]
        if need_ctx:
            x_ctx = outs[1]
    return x_lat[None]
```

```python
import functools

import numpy as np
import jax
import jax.numpy as jnp
from jax import lax
from jax.experimental import pallas as pl
from jax.experimental.pallas import tpu as pltpu

F32 = jnp.float32
BF16 = jnp.bfloat16

D_MODEL = 1024
DEPTH = 4
GRID_W = 64
HEAD_DIM = 64
EPS = 1e-6
A_HEADS = 8
A_KV_HEADS = 2
A_Q = A_HEADS * HEAD_DIM
A_KV = A_KV_HEADS * HEAD_DIM
A_IN = A_Q + 2 * A_KV
ROPE_THETA = 10000.0
ROPE_FREQS = HEAD_DIM // 4
B_GROUPS = 8
B_WIDTH = 512
CHUNK = 128
C_HEADS = 8
C_W = C_HEADS * HEAD_DIM
NA_ROWS = 8
NA_COLS = 16
D_CH = 512
CONV_W = 31
EVEN_IN = A_IN + 2 * B_WIDTH
ODD_IN = 3 * C_W + 2 * D_CH
N_EXPERTS = 16
N_GROUPS = 4
EXPERTS_PER_GROUP = 4
D_EXPERT = 512
MOE_BLOCK = 256

LANES = 128
SUBLANES = 8
MXU_WIDTH = 256
HALF = 64
NEG_BIG = -1e30
CONV_HALO = 16
DMA_UNROLL = 8
ATTN_TQ = 512
ATTN_TK = 512
QK_SCALE_LOG2 =HEAD_DIM ** -0.5 * 1.4426950408889634


def _cparams(n_axes, vmem_mb):
    return pltpu.CompilerParams(dimension_semantics=("arbitrary",) * n_axes,
                                vmem_limit_bytes=vmem_mb << 20)


def _full(shape):
    nd = len(shape)
    return pl.BlockSpec(shape, lambda *_: (0,) * nd)


def _resident(shape):
    nd = len(shape)
    return pl.BlockSpec(shape, lambda *_: (0,) * nd, pipeline_mode=pl.Buffered(1))


def _dot(a, b):
    return jnp.dot(a, b, preferred_element_type=F32)


def _dot_nt(a, b):
    return lax.dot_general(a, b, (((1,), (1,)), ((), ())), preferred_element_type=F32)


def _split_bf16(x):
    hi = x.astype(BF16)
    lo = (x - hi.astype(F32)).astype(BF16)
    return hi, lo


def _seg_mean_sq(x, bd):
    hi, lo = _split_bf16(x * x)
    group = min(MXU_WIDTH, x.shape[1])
    bdg = bd[:group, :group]
    parts = [_dot(hi[:, c:c + group], bdg) + _dot(lo[:, c:c + group], bdg) for c in range(0, x.shape[1], group)]
    return (parts[0] if len(parts) == 1 else jnp.concatenate(parts, axis=1)) * (1.0 / HALF)


def _rms_mod(x, g, sc, sh):
    ms = jnp.mean(x * x, axis=-1, keepdims=True)
    return x * lax.rsqrt(ms + EPS) * g * (1.0 + sc) + sh


def _rope(x, c, sg, lane):
    sw = jnp.where((lane % 32) < 16, pltpu.roll(x, LANES - 16, 1), pltpu.roll(x, 16, 1))
    return x * c + sw * sg


def _pair_lhs(qs, lane_row):
    ma = jnp.where(lane_row < HALF, 1.0, 0.0).astype(BF16)
    mb = jnp.where(lane_row < HALF, 0.0, 1.0).astype(BF16)
    return jnp.concatenate([qs * ma, qs * mb], axis=0)


def _store_row_tiles(ref, val, t, row0=0):
    for k in range(SUBLANES):
        ref[pl.ds(row0 * SUBLANES + k, t, stride=SUBLANES), :] = val[:, k * LANES:(k + 1) * LANES]


def _load_row_tiles(ref, t, row0=0):
    return jnp.concatenate([ref[pl.ds(row0 * SUBLANES + k, t, stride=SUBLANES), :] for k in range(SUBLANES)], axis=1)


def _mod_kernel(cc_ref, w_ref, b_ref, o_ref):
    cc = cc_ref[...]
    s = cc * jax.nn.sigmoid(cc)
    sh, sl = _split_bf16(s)
    wh, wl = _split_bf16(w_ref[0])
    o_ref[0] = _dot(sh, wh) + _dot(sl, wh) + _dot(sh, wl) + b_ref[0]


def _modulation(cc, ada_w, ada_b):
    tn = 1024
    return pl.pallas_call(
        _mod_kernel,
        grid=(DEPTH, 6 * D_MODEL // tn),
        in_specs=[pl.BlockSpec((8, D_MODEL), lambda l, j: (0, 0)),
                  pl.BlockSpec((1, D_MODEL, tn), lambda l, j: (l, 0, j)),
                  pl.BlockSpec((1, 1, tn), lambda l, j: (l, 0, j))],
        out_specs=pl.BlockSpec((1, 8, tn), lambda l, j: (l, 0, j)),
        out_shape=jax.ShapeDtypeStruct((DEPTH, 8, 6 * D_MODEL), F32),
        compiler_params=_cparams(2, 32),
        name="modulation",
    )(cc, ada_w, ada_b.reshape(DEPTH, 1, 6 * D_MODEL))


def _proj_even_kernel(x_ref, g_ref, sc_ref, sh_ref, w_ref, bd512_ref, bd128_ref, qg_ref, kg_ref, vg_ref,
                      cos_ref, sin_ref, ws_ref, bsb_ref, q_ref, kd_ref, v_ref, b_ref, *, ts):
    h = _rms_mod(x_ref[...], g_ref[...], sc_ref[...], sh_ref[...])
    p = _dot(h.astype(BF16), w_ref[...])
    lane = lax.broadcasted_iota(jnp.int32, (ts, LANES), 1)
    c = cos_ref[...]
    sg = sin_ref[...]

    q = p[:, :A_Q]
    qn = q * lax.rsqrt(_seg_mean_sq(q, bd512_ref[...]) + EPS) * qg_ref[...]
    for sl in range(A_Q // LANES):
        xs = qn[:, sl * LANES:(sl + 1) * LANES]
        q_ref[:, sl * LANES:(sl + 1) * LANES] = (_rope(xs, c, sg, lane) * QK_SCALE_LOG2).astype(BF16)

    k = p[:, A_Q:A_Q + A_KV]
    kn = k * lax.rsqrt(_seg_mean_sq(k, bd128_ref[...]) + EPS) * kg_ref[...]
    kt = _rope(kn, c, sg, lane).T
    for j in range(A_KV_HEADS):
        kj = kt[j * HALF:(j + 1) * HALF]
        kd_ref[j] = jnp.concatenate([kj, kj], axis=0).astype(BF16)
    v = p[:, A_Q + A_KV:A_IN]
    v_ref[0] = jnp.where(lane < HALF, v, 1.0).astype(BF16)
    v_ref[1] = jnp.where(lane < HALF, pltpu.roll(v, HALF, 1), 1.0).astype(BF16)

    zu = jax.nn.gelu(p[:, A_IN:A_IN + B_WIDTH])
    zv = jax.nn.gelu(p[:, A_IN + B_WIDTH:])
    vn = (zv * lax.rsqrt(_seg_mean_sq(zv, bd512_ref[...]) + EPS) * vg_ref[...]).astype(BF16)
    lane_c = lax.broadcasted_iota(jnp.int32, (CHUNK, LANES), 1)
    for ch in range(ts // CHUNK):
        rows = slice(ch * CHUNK, (ch + 1) * CHUNK)
        for sl in range(B_WIDTH // LANES):
            cols = slice(sl * LANES, (sl + 1) * LANES)
            vs = vn[rows, cols]
            both = _dot(ws_ref[sl], vs)
            mixed = jnp.where(lane_c < HALF, both[:CHUNK], both[CHUNK:])
            b_ref[rows, cols] = (zu[rows, cols] * (mixed + bsb_ref[:, cols])).astype(BF16)


def _proj_even(x, mods, w_in, consts, cos, sin, ts):
    n = x.shape[0]
    g, sc, sh = mods
    bd512, bd128, qg, kg, vg, ws, bsb = consts
    row = lambda i: (i, 0)
    vec = lambda w: pl.BlockSpec((1, w), lambda i: (0, 0))
    return pl.pallas_call(
        functools.partial(_proj_even_kernel, ts=ts),
        grid=(n // ts,),
        in_specs=[pl.BlockSpec((ts, D_MODEL), row), vec(D_MODEL), vec(D_MODEL), vec(D_MODEL),
                  _full((D_MODEL, EVEN_IN)), _full((A_Q, A_Q)), _full((LANES, LANES)),
                  vec(A_Q), vec(A_KV), vec(B_WIDTH),
                  pl.BlockSpec((ts, LANES), row), pl.BlockSpec((ts, LANES), row),
                  _full((B_GROUPS // 2, 2 * CHUNK, CHUNK)), _full((CHUNK, B_WIDTH))],
        out_specs=[pl.BlockSpec((ts, A_Q), row),
                   pl.BlockSpec((A_KV_HEADS, LANES, ts), lambda i: (0, 0, i)),
                   pl.BlockSpec((A_KV_HEADS, ts, LANES), lambda i: (0, i, 0)),
                   pl.BlockSpec((ts, B_WIDTH), row)],
        out_shape=[jax.ShapeDtypeStruct((n, A_Q), BF16),
                   jax.ShapeDtypeStruct((A_KV_HEADS, LANES, n), BF16),
                   jax.ShapeDtypeStruct((A_KV_HEADS, n, LANES), BF16),
                   jax.ShapeDtypeStruct((n, B_WIDTH), BF16)],
        compiler_params=_cparams(1, 56),
        name="proj_even",
    )(x, g, sc, sh, w_in, bd512, bd128, qg, kg, vg, cos, sin, ws, bsb)


def _attn_even_kernel(q_ref, kd_ref, va_ref, o_ref, lhs_scr, s_a, s_b, mx_a, mx_b, m_scr, acc_scr,
                      *, tq, tk, n_chunks):
    rows = 4 * tq
    lane_row = lax.broadcasted_iota(jnp.int32, (1, LANES), 1)
    lane = lax.broadcasted_iota(jnp.int32, (tq, LANES), 1)
    for j in range(A_KV_HEADS):
        lhs_scr[...] = jnp.concatenate([_pair_lhs(q_ref[:, sl * LANES:(sl + 1) * LANES], lane_row)
                                        for sl in (2 * j, 2 * j + 1)], axis=0)
        m_scr[...] = jnp.full((rows, LANES), NEG_BIG, F32)
        acc_scr[...] = jnp.zeros((rows, LANES), F32)

        def scores(c, s_scr, mx_scr, j=j):
            off = pl.multiple_of(c * tk, tk)
            s = _dot(lhs_scr[...], kd_ref[j, :, pl.ds(off, tk)])
            s_scr[...] = s
            mx_scr[...] = jnp.broadcast_to(jnp.max(s, axis=-1, keepdims=True), (rows, LANES))

        def accumulate(c, s_scr, mx_scr, j=j):
            off = pl.multiple_of(c * tk, tk)
            m_old = m_scr[...]
            m_new = jnp.maximum(m_old, mx_scr[...])
            p = jnp.concatenate([jnp.exp2(s_scr[:, t * LANES:(t + 1) * LANES] - m_new)
                                 for t in range(tk // LANES)], axis=1).astype(BF16)
            pv = _dot(p, va_ref[j, pl.ds(off, tk), :])
            acc_scr[...] = jnp.exp2(m_old - m_new) * acc_scr[...] + pv
            m_scr[...] = m_new

        scores(0, s_a, mx_a)

        def body(i, carry):
            scores(2 * i + 1, s_b, mx_b)
            accumulate(2 * i, s_a, mx_a)
            scores(2 * i + 2, s_a, mx_a)
            accumulate(2 * i + 1, s_b, mx_b)
            return carry

        lax.fori_loop(0, (n_chunks - 1) // 2, body, 0, unroll=2)
        accumulate(n_chunks - 1, s_a, mx_a)
        acc = acc_scr[...]
        o = acc / pltpu.roll(acc, HALF, 1)
        for t, sl in enumerate((2 * j, 2 * j + 1)):
            oa = o[(2 * t) * tq:(2 * t + 1) * tq]
            ob = o[(2 * t + 1) * tq:(2 * t + 2) * tq]
            o_ref[:, sl * LANES:(sl + 1) * LANES] = jnp.where(lane < HALF, oa, pltpu.roll(ob, HALF, 1)).astype(BF16)


def _attn_even(q, kd, va, tq, tk):
    n = q.shape[0]
    n_chunks = kd.shape[2] // tk
    assert n_chunks % 2 == 1
    rows = 4 * tq
    big = lambda w: pltpu.VMEM((rows, w), F32)
    return pl.pallas_call(
        functools.partial(_attn_even_kernel, tq=tq, tk=tk, n_chunks=n_chunks),
        grid=(n // tq,),
        in_specs=[pl.BlockSpec((tq, A_Q), lambda i: (i, 0)), _resident(kd.shape), _resident(va.shape)],
        out_specs=pl.BlockSpec((tq, A_Q), lambda i: (i, 0)),
        out_shape=jax.ShapeDtypeStruct((n, A_Q), BF16),
        scratch_shapes=[pltpu.VMEM((rows, LANES), BF16), big(tk), big(tk), big(LANES), big(LANES),
                        big(LANES), big(LANES)],
        compiler_params=_cparams(1, 56),
        name="attn_even",
    )(q, kd, va)


def _proj_odd_kernel(x_ref, g_ref, sc_ref, sh_ref, w_ref, bd512_ref, qg_ref, kg_ref,
                     q_ref, k_ref, v_ref, y_ref):
    h = _rms_mod(x_ref[...], g_ref[...], sc_ref[...], sh_ref[...])
    p = _dot(h.astype(BF16), w_ref[...])
    bd = bd512_ref[...]
    q = p[:, :C_W]
    q_ref[...] = (q * lax.rsqrt(_seg_mean_sq(q, bd) + EPS) * qg_ref[...] * (HEAD_DIM ** -0.5)).astype(BF16)
    k = p[:, C_W:2 * C_W]
    k_ref[...] = (k * lax.rsqrt(_seg_mean_sq(k, bd) + EPS) * kg_ref[...]).astype(BF16)
    v_ref[...] = p[:, 2 * C_W:3 * C_W].astype(BF16)
    y_ref[...] = p[:, 3 * C_W:3 * C_W + D_CH] * jax.nn.sigmoid(p[:, 3 * C_W + D_CH:])


def _proj_odd(x, mods, w_in, consts, ts):
    n = x.shape[0]
    g, sc, sh = mods
    bd512, qg, kg = consts
    row = lambda i: (i, 0)
    vec = lambda w: pl.BlockSpec((1, w), lambda i: (0, 0))
    blk = pl.BlockSpec((ts, C_W), row)
    return pl.pallas_call(
        _proj_odd_kernel,
        grid=(n // ts,),
        in_specs=[pl.BlockSpec((ts, D_MODEL), row), vec(D_MODEL), vec(D_MODEL), vec(D_MODEL),
                  _full((D_MODEL, ODD_IN)), _full((C_W, C_W)), vec(C_W), vec(C_W)],
        out_specs=[blk, blk, blk, blk],
        out_shape=[jax.ShapeDtypeStruct((n, C_W), BF16), jax.ShapeDtypeStruct((n, C_W), BF16),
                   jax.ShapeDtypeStruct((n, C_W), BF16), jax.ShapeDtypeStruct((n, D_CH), F32)],
        compiler_params=_cparams(1, 56),
        name="proj_odd",
    )(x, g, sc, sh, w_in, bd512, qg, kg)


NA_BLOCK_ROWS = 8
NA_TOK = NA_BLOCK_ROWS * GRID_W
NA_WIN = NA_ROWS * GRID_W


def _na_kernel(q_ref, kp_ref, kc_ref, kn_ref, vp_ref, vc_ref, vn_ref, kx_ref, vx_ref, bias_ref, o_ref,
               kbuf, vbuf, *, rows):
    i = pl.program_id(0)
    for t, (kr, vr) in enumerate(((kp_ref, vp_ref), (kc_ref, vc_ref), (kn_ref, vn_ref))):
        kbuf[t * NA_TOK:(t + 1) * NA_TOK] = kr[...]
        vbuf[t * NA_TOK:(t + 1) * NA_TOK] = vr[...]
    lane_row = lax.broadcasted_iota(jnp.int32, (1, LANES), 1)
    lane = lax.broadcasted_iota(jnp.int32, (GRID_W, LANES), 1)
    for sl in range(C_W // LANES):
        cols = slice(sl * LANES, (sl + 1) * LANES)
        lhs = _pair_lhs(q_ref[:, cols], lane_row)
        s_ctx = _dot_nt(lhs, kx_ref[:, cols])
        sa, sb, starts = [], [], []
        for j in range(NA_BLOCK_ROWS):
            r = i * NA_BLOCK_ROWS + j
            r0 = jnp.clip(r - NA_ROWS // 2, 0, rows - NA_ROWS)
            start = pl.multiple_of((r0 - (i - 1) * NA_BLOCK_ROWS) * GRID_W, GRID_W)
            starts.append(start)
            qa = lhs[j * GRID_W:(j + 1) * GRID_W]
            qb = lhs[NA_TOK + j * GRID_W:NA_TOK + (j + 1) * GRID_W]
            kw = kbuf[pl.ds(start, NA_WIN), cols]
            s = _dot_nt(jnp.concatenate([qa, qb], axis=0), kw) + bias_ref[r0 - r + NA_ROWS - 1, sl]
            sa.append(s[:GRID_W])
            sb.append(s[GRID_W:])
        s_loc = jnp.concatenate(sa + sb, axis=0)
        m = jnp.maximum(jnp.max(s_loc, axis=-1, keepdims=True), jnp.max(s_ctx, axis=-1, keepdims=True))
        p_loc = jnp.exp(s_loc - m)
        p_ctx = jnp.exp(s_ctx - m)
        l = jnp.sum(p_loc, axis=-1, keepdims=True) + jnp.sum(p_ctx, axis=-1, keepdims=True)
        o_ctx = _dot(p_ctx.astype(BF16), vx_ref[:, cols])
        p_loc = p_loc.astype(BF16)
        for j in range(NA_BLOCK_ROWS):
            ra = slice(j * GRID_W, (j + 1) * GRID_W)
            rb = slice(NA_TOK + j * GRID_W, NA_TOK + (j + 1) * GRID_W)
            vw = vbuf[pl.ds(starts[j], NA_WIN), cols]
            o2 = _dot(jnp.concatenate([p_loc[ra], p_loc[rb]], axis=0), vw)
            oa = (o2[:GRID_W] + o_ctx[ra]) / l[ra]
            ob = (o2[GRID_W:] + o_ctx[rb]) / l[rb]
            o_ref[ra, cols] = jnp.where(lane < HALF, oa, ob).astype(BF16)


def _na_attention(q, k, v, kx, vx, bias):
    n = q.shape[0]
    nb = n // NA_TOK
    blk = lambda f: pl.BlockSpec((NA_TOK, C_W), f)
    prev = lambda i: (jnp.maximum(i - 1, 0), 0)
    cur = lambda i: (i, 0)
    nxt = lambda i: (jnp.minimum(i + 1, nb - 1), 0)
    return pl.pallas_call(
        functools.partial(_na_kernel, rows=n // GRID_W),
        grid=(nb,),
        in_specs=[blk(cur), blk(prev), blk(cur), blk(nxt), blk(prev), blk(cur), blk(nxt),
                  _full(kx.shape), _full(vx.shape), _full(bias.shape)],
        out_specs=blk(cur),
        out_shape=jax.ShapeDtypeStruct((n, C_W), BF16),
        scratch_shapes=[pltpu.VMEM((3 * NA_TOK, C_W), BF16), pltpu.VMEM((3 * NA_TOK, C_W), BF16)],
        compiler_params=_cparams(1, 56),
        name="na_attention",
    )(q, k, k, k, v, v, v, kx, vx, bias)


def _na_bias(rpb):
    cc = np.arange(GRID_W)
    c0 = np.clip(cc - NA_COLS // 2, 0, GRID_W - NA_COLS)
    kc = np.arange(GRID_W)
    inwin = (kc[None, :] >= c0[:, None]) & (kc[None, :] < c0[:, None] + NA_COLS)
    dc = np.clip(kc[None, :] - cc[:, None] + NA_COLS - 1, 0, 2 * NA_COLS - 2)
    n_dr, n_dc = 2 * NA_ROWS - 1, 2 * NA_COLS - 1
    sel = np.zeros((n_dc, GRID_W * GRID_W), np.float32)
    sel[dc.ravel(), np.arange(GRID_W * GRID_W)] = inwin.ravel()
    t2 = jnp.dot(rpb.astype(F32).reshape(C_HEADS * n_dr, n_dc), jnp.asarray(sel), precision=lax.Precision.HIGHEST)
    t2 = t2.reshape(C_HEADS, n_dr, GRID_W, GRID_W) + jnp.asarray(np.where(inwin, 0.0, NEG_BIG), F32)
    t = jnp.stack([t2[:, oi:oi + NA_ROWS] for oi in range(NA_ROWS)], axis=0)
    t = jnp.transpose(t, (0, 1, 3, 2, 4))
    return t.reshape(NA_ROWS, C_HEADS // 2, 2 * GRID_W, NA_WIN)


def _mha_kernel(q_ref, k_ref, v_ref, o_ref):
    n = q_ref.shape[0]
    lane_row = lax.broadcasted_iota(jnp.int32, (1, LANES), 1)
    lane = lax.broadcasted_iota(jnp.int32, (n, LANES), 1)
    for sl in range(C_W // LANES):
        cols = slice(sl * LANES, (sl + 1) * LANES)
        s = _dot_nt(_pair_lhs(q_ref[:, cols], lane_row), k_ref[:, cols])
        p = jnp.exp(s - jnp.max(s, axis=-1, keepdims=True))
        l = jnp.sum(p, axis=-1, keepdims=True)
        o = _dot(p.astype(BF16), v_ref[:, cols]) / l
        o_ref[:, cols] = jnp.where(lane < HALF, o[:n], o[n:]).astype(BF16)


def _mha_small(q, k, v):
    return pl.pallas_call(
        _mha_kernel,
        grid=(1,),
        in_specs=[_full(q.shape), _full(k.shape), _full(v.shape)],
        out_specs=_full(q.shape),
        out_shape=jax.ShapeDtypeStruct(q.shape, BF16),
        compiler_params=_cparams(1, 32),
        name="mha_small",
    )(q, k, v)


def _conv_kernel(y_ref, yp_ref, yn_ref, dw_ref, dwb_ref, lng_ref, lnb_ref, o_ref, ext, shifted, *, ts):
    i = pl.program_id(0)
    nb = pl.num_programs(0)
    ext[0:CONV_HALO] = jnp.where(i > 0, yp_ref[...], 0.0)
    ext[CONV_HALO:CONV_HALO + ts] = y_ref[...]
    ext[CONV_HALO + ts:] = jnp.where(i < nb - 1, yn_ref[...], 0.0)
    base = CONV_HALO - CONV_W // 2
    span = ts + (base + CONV_W - 1) // SUBLANES * SUBLANES
    for s in range(SUBLANES):
        shifted[s] = ext[s:s + span, :]
    acc = jnp.zeros((ts, D_CH), F32)
    for j in range(CONV_W):
        s, start = (base + j) % SUBLANES, (base + j) // SUBLANES * SUBLANES
        acc = acc + shifted[s, start:start + ts, :] * dw_ref[j:j + 1, :]
    yb = acc + dwb_ref[...]
    mu = jnp.mean(yb, axis=-1, keepdims=True)
    var = jnp.mean(jnp.square(yb - mu), axis=-1, keepdims=True)
    z = (yb - mu) * lax.rsqrt(var + EPS) * lng_ref[...] + lnb_ref[...]
    o_ref[...] = (z * jax.nn.sigmoid(z)).astype(BF16)


def _conv_module(y, dw, dwb, lng, lnb, ts):
    n = y.shape[0]
    hb = ts // CONV_HALO
    nh = n // CONV_HALO
    vec = pl.BlockSpec((1, D_CH), lambda i: (0, 0))
    return pl.pallas_call(
        functools.partial(_conv_kernel, ts=ts),
        grid=(n // ts,),
        in_specs=[pl.BlockSpec((ts, D_CH), lambda i: (i, 0)),
                  pl.BlockSpec((CONV_HALO, D_CH), lambda i: (jnp.maximum(i * hb - 1, 0), 0)),
                  pl.BlockSpec((CONV_HALO, D_CH), lambda i: (jnp.minimum((i + 1) * hb, nh - 1), 0)),
                  _full((CONV_W + 1, D_CH)), vec, vec, vec],
        out_specs=pl.BlockSpec((ts, D_CH), lambda i: (i, 0)),
        out_shape=jax.ShapeDtypeStruct((n, D_CH), BF16),
        scratch_shapes=[pltpu.VMEM((ts + 2 * CONV_HALO, D_CH), F32),
                        pltpu.VMEM((SUBLANES, ts + 2 * CONV_HALO - SUBLANES, D_CH), F32)],
        compiler_params=_cparams(1, 48),
        name="conv_module",
    )(y, y, y, dw, dwb, lng, lnb)


def _first_max2(vals):
    def first_max(vs):
        m = functools.reduce(jnp.maximum, vs)
        idx = len(vs) - 1
        for k in range(len(vs) - 2, -1, -1):
            idx = jnp.where(vs[k] == m, k, idx)
        return m, idx

    m1, i1 = first_max(vals)
    m2, i2 = first_max([jnp.where(i1 == k, -jnp.inf, v) for k, v in enumerate(vals)])
    return m1, m2, i1, i2


def _pick(idx, vals):
    out = vals[-1]
    for k in range(len(vals) - 2, -1, -1):
        out = jnp.where(idx == k, vals[k], out)
    return out


def _route_t(lt, rbc, tri, carry, ts):
    nk = EXPERTS_PER_GROUP
    aff = jax.nn.sigmoid(lt[:nk * SUBLANES])
    sel = aff + rbc[:nk * SUBLANES]
    sel_k = [sel[k * SUBLANES:(k + 1) * SUBLANES] for k in range(nk)]
    aff_k = [aff[k * SUBLANES:(k + 1) * SUBLANES] for k in range(nk)]
    m1, m2, _, _ = _first_max2(sel_k)
    gscore = m1 + m2
    best, bg = gscore[0:1], jnp.zeros((1, ts), jnp.int32)
    for g in range(1, N_GROUPS):
        upd = gscore[g:g + 1] > best
        best = jnp.where(upd, gscore[g:g + 1], best)
        bg = jnp.where(upd, g, bg)
    sel_b = [_pick(bg, [v[g:g + 1] for g in range(N_GROUPS)]) for v in sel_k]
    aff_b = [_pick(bg, [v[g:g + 1] for g in range(N_GROUPS)]) for v in aff_k]
    _, _, i1, i2 = _first_max2(sel_b)
    a1, a2 = _pick(i1, aff_b), _pick(i2, aff_b)
    e1 = bg * nk + i1
    e2 = bg * nk + i2
    erow = lax.broadcasted_iota(jnp.int32, (N_EXPERTS, ts), 0)
    hit1 = erow == e1
    hit2 = erow == e2
    onehot = jnp.where(hit1 | hit2, 1.0, 0.0)
    before = _dot(onehot.astype(BF16), tri) + carry[:, 0:1]
    r1 = jnp.sum(jnp.where(hit1, before, 0.0), axis=0, keepdims=True)
    r2 = jnp.sum(jnp.where(hit2, before, 0.0), axis=0, keepdims=True)
    new_carry = carry + jnp.sum(onehot, axis=1, keepdims=True)
    tot = a1 + a2
    rows = [e1.astype(F32), e2.astype(F32), a1 / tot, a2 / tot, r1, r2, jnp.zeros((2, ts), F32)]
    return jnp.concatenate(rows, axis=0), new_carry


def _post_mix_kernel(x_ref, a_ref, b_ref, w_ref, g1_ref, ng_ref, sc_ref, sh_ref, rwh_ref, rwl_ref, rb_ref, tri_ref,
                     cnt0_ref, xo_ref, h2_ref, route_ref, cnt_ref, *, ts):
    @pl.when(pl.program_id(0) == 0)
    def _():
        cnt_ref[...] = cnt0_ref[...]

    half = w_ref.shape[0] // 2
    mo = _dot(a_ref[...], w_ref[:half]) + _dot(b_ref[...], w_ref[half:])
    xn = x_ref[...] + g1_ref[...] * mo
    xo_ref[...] = xn
    h2 = _rms_mod(xn, ng_ref[...], sc_ref[...], sh_ref[...])
    _store_row_tiles(h2_ref, h2, ts)
    hh, hl = _split_bf16(h2)
    lt = _dot_nt(rwh_ref[...], hh) + _dot_nt(rwh_ref[...], hl) + _dot_nt(rwl_ref[...], hh)
    rt, cnt = _route_t(lt, rb_ref[...], tri_ref[...], cnt_ref[...], ts)
    cnt_ref[...] = cnt
    route_ref[...] = jnp.concatenate([rt, jnp.zeros((LANES - SUBLANES, ts), F32)], axis=0).T


def _post_mix(x, a, b, w_out, mods, router, cnt0, ts):
    n = x.shape[0]
    g1, ng, sc, sh = mods
    rwh, rwl, rbc = router
    tri = jnp.asarray(np.triu(np.ones((ts, ts), np.float32), 1), dtype=BF16)
    row = lambda i: (i, 0)
    vec = lambda w: pl.BlockSpec((1, w), lambda i: (0, 0))
    return pl.pallas_call(
        functools.partial(_post_mix_kernel, ts=ts),
        grid=(n // ts,),
        in_specs=[pl.BlockSpec((ts, D_MODEL), row), pl.BlockSpec((ts, a.shape[1]), row),
                  pl.BlockSpec((ts, b.shape[1]), row), _full(w_out.shape),
                  vec(D_MODEL), vec(D_MODEL), vec(D_MODEL), vec(D_MODEL),
                  _full(rwh.shape), _full(rwl.shape), _full(rbc.shape), _full(tri.shape),
                  _full((N_EXPERTS, LANES))],
        out_specs=[pl.BlockSpec((ts, D_MODEL), row), pl.BlockSpec((ts * SUBLANES, LANES), row),
                   pl.BlockSpec((ts, LANES), row), _full((N_EXPERTS, LANES))],
        out_shape=[jax.ShapeDtypeStruct((n, D_MODEL), F32), jax.ShapeDtypeStruct((n * SUBLANES, LANES), F32),
                   jax.ShapeDtypeStruct((n, LANES), F32), jax.ShapeDtypeStruct((N_EXPERTS, LANES), F32)],
        compiler_params=_cparams(1, 56),
        name="post_mix",
    )(x, a, b, w_out, g1, ng, sc, sh, rwh, rwl, rbc, tri, cnt0)


def _start_tile_copies(make_copy, ts):
    def issue(i, carry):
        copies = [make_copy(i * DMA_UNROLL + u, k) for u in range(DMA_UNROLL) for k in range(2)]
        for j, cp in enumerate(copies):
            cp.start(priority=j % 2)
        return carry

    lax.fori_loop(0, ts // DMA_UNROLL, issue, 0)


def _wait_tile_copies(make_copy, ts):
    def drain(i, carry):
        for u in range(DMA_UNROLL):
            make_copy(i * DMA_UNROLL + u, 0).wait()
            make_copy(i * DMA_UNROLL + u, 1).wait()
        return carry

    lax.fori_loop(0, ts // DMA_UNROLL, drain, 0)


def _dispatch_kernel(pend_ref, dest_ref, *rest, tss, tile_offsets):
    n_streams = len(tile_offsets) - 1
    h_refs = rest[:n_streams]
    buf_ref, zeros_scr, zsem, sem = rest[n_streams:]
    i = pl.program_id(0)
    pl.when(i == 0)(functools.partial(_define_padding, pend_ref, buf_ref, zeros_scr, zsem))
    for s, h_ref in enumerate(h_refs):

        def row_copy(r, k, h_ref=h_ref):
            src = h_ref.at[pl.ds(pl.multiple_of(r * SUBLANES, SUBLANES), SUBLANES)]
            dst = buf_ref.at[pl.ds(pl.multiple_of(dest_ref[0, 0, 2 * r + k], SUBLANES), SUBLANES)]
            return pltpu.make_async_copy(src, dst, sem)

        @pl.when((i >= tile_offsets[s]) & (i < tile_offsets[s + 1]))
        def _(row_copy=row_copy, ts=tss[s]):
            _start_tile_copies(row_copy, ts)
            _wait_tile_copies(row_copy, ts)


def _define_padding(pend_ref, buf_ref, zeros_scr, zsem):
    zeros_scr[...] = jnp.zeros_like(zeros_scr)
    block_rows = MOE_BLOCK * SUBLANES

    def zero_block(first_row):
        return pltpu.make_async_copy(zeros_scr, buf_ref.at[pl.ds(pl.multiple_of(first_row, SUBLANES), block_rows)], zsem)

    def last_block(e):
        return zero_block((pend_ref[e] - MOE_BLOCK) * SUBLANES)

    def nonempty(e):
        return pend_ref[e] > (pend_ref[e - 1] if e > 0 else 0)

    for e in range(N_EXPERTS):
        pl.when(nonempty(e))(lambda e=e: last_block(e).start())
    for e in range(N_EXPERTS):
        pl.when(nonempty(e))(lambda e=e: last_block(e).wait())
    n_used = pend_ref[N_EXPERTS - 1] // MOE_BLOCK
    n_blocks = buf_ref.shape[0] // block_rows
    lax.fori_loop(n_used, n_blocks, lambda b, c: (zero_block(b * block_rows).start(), c)[1], 0)
    lax.fori_loop(n_used, n_blocks, lambda b, c: (zero_block(b * block_rows).wait(), c)[1], 0)


def _dispatch(h2ts, dests, tss, pend, n_rows):
    tiles = [h.shape[0] // (SUBLANES * ts) for h, ts in zip(h2ts, tss)]
    offs = [sum(tiles[:s]) for s in range(len(tiles) + 1)]
    width = 2 * max(tss)
    dest3 = jnp.concatenate([jnp.pad(d.reshape(t, 2 * ts), ((0, 0), (0, width - 2 * ts)))
                             for d, t, ts in zip(dests, tiles, tss)], axis=0).reshape(offs[-1], 1, width)
    in_specs = [pl.BlockSpec((1, 1, width), lambda i, pe: (i, 0, 0), memory_space=pltpu.SMEM)]
    for s in range(len(h2ts)):
        in_specs.append(pl.BlockSpec((tss[s] * SUBLANES, LANES),
                                     lambda i, pe, s=s: (jnp.clip(i - offs[s], 0, tiles[s] - 1), 0)))
    grid_spec = pltpu.PrefetchScalarGridSpec(
        num_scalar_prefetch=1, grid=(offs[-1],), in_specs=in_specs,
        out_specs=pl.BlockSpec(memory_space=pl.ANY),
        scratch_shapes=[pltpu.VMEM((MOE_BLOCK * SUBLANES, LANES), F32), pltpu.SemaphoreType.DMA(()),
                        pltpu.SemaphoreType.DMA(())])
    return pl.pallas_call(
        functools.partial(_dispatch_kernel, tss=tuple(tss), tile_offsets=tuple(offs)),
        grid_spec=grid_spec,
        out_shape=jax.ShapeDtypeStruct((n_rows * SUBLANES, LANES), F32),
        compiler_params=_cparams(1, 32),
        name="moe_dispatch",
    )(pend, dest3, *h2ts)


def _expert_kernel(blk_e_ref, n_used_ref, x_ref, wg_ref, wu_ref, wd_ref, y_ref, wg_s, wu_s, wd_s):
    i = pl.program_id(0)

    @pl.when((i == 0) | (blk_e_ref[i] != blk_e_ref[jnp.maximum(i - 1, 0)]))
    def _():
        wg_s[...] = wg_ref[0, 0].astype(BF16)
        wu_s[...] = wu_ref[0, 0].astype(BF16)
        wd_s[...] = wd_ref[0, 0].astype(BF16)

    @pl.when(i < n_used_ref[0])
    def _():
        xb = _load_row_tiles(x_ref, MOE_BLOCK).astype(BF16)
        gate = _dot(xb, wg_s[...])
        up = _dot(xb, wu_s[...])
        hmid = gate * jax.nn.sigmoid(gate) * up
        _store_row_tiles(y_ref, _dot(hmid.astype(BF16), wd_s[...]), MOE_BLOCK)

    @pl.when(i >= n_used_ref[0])
    def _():
        y_ref[...] = jnp.zeros_like(y_ref)


def _experts(buf, blk_e, n_used, layer, wg, wu, wd):
    nb = buf.shape[0] // (MOE_BLOCK * SUBLANES)
    used = lambda i, be, nu: (jnp.minimum(i, nu[0] - 1), 0)
    wsel = lambda i, be, nu: (layer, be[i], 0, 0)
    grid_spec = pltpu.PrefetchScalarGridSpec(
        num_scalar_prefetch=2,
        grid=(nb,),
        in_specs=[pl.BlockSpec((MOE_BLOCK * SUBLANES, LANES), used),
                  pl.BlockSpec((1, 1, D_MODEL, D_EXPERT), wsel),
                  pl.BlockSpec((1, 1, D_MODEL, D_EXPERT), wsel),
                  pl.BlockSpec((1, 1, D_EXPERT, D_MODEL), wsel)],
        out_specs=pl.BlockSpec((MOE_BLOCK * SUBLANES, LANES), lambda i, be, nu: (i, 0)),
        scratch_shapes=[pltpu.VMEM((D_MODEL, D_EXPERT), BF16), pltpu.VMEM((D_MODEL, D_EXPERT), BF16),
                        pltpu.VMEM((D_EXPERT, D_MODEL), BF16)],
    )
    return pl.pallas_call(
        _expert_kernel,
        grid_spec=grid_spec,
        out_shape=jax.ShapeDtypeStruct(buf.shape, F32),
        compiler_params=_cparams(1, 48),
        name="moe_experts",
    )(blk_e, n_used, buf, wg, wu, wd)


def _combine_kernel(dest_ref, dest_next_ref, x_ref, route_ref, g2_ref, y_ref, o_ref, gbuf, sems, *, ts):
    i = pl.program_id(0)
    nt = pl.num_programs(0)

    def gather(d_ref, slot):
        def row_copy(r, k):
            src = y_ref.at[pl.ds(pl.multiple_of(d_ref[0, 0, 2 * r + k], SUBLANES), SUBLANES)]
            dst = gbuf.at[slot, k, pl.ds(pl.multiple_of(r * SUBLANES, SUBLANES), SUBLANES)]
            return pltpu.make_async_copy(src, dst, sems.at[slot])
        return row_copy

    slot = i % 2
    pl.when(i == 0)(lambda: _start_tile_copies(gather(dest_ref, 0), ts))
    pl.when(i + 1 < nt)(lambda: _start_tile_copies(gather(dest_next_ref, 1 - slot), ts))
    _wait_tile_copies(gather(dest_ref, slot), ts)
    route = route_ref[...]
    w0, w1 = route[:, 2:3], route[:, 3:4]
    for k in range(SUBLANES):
        cols = slice(k * LANES, (k + 1) * LANES)
        f = (gbuf[slot, 0, pl.ds(k, ts, stride=SUBLANES), :] * w0
             + gbuf[slot, 1, pl.ds(k, ts, stride=SUBLANES), :] * w1)
        o_ref[:, cols] = x_ref[:, cols] + g2_ref[:, cols] * f


def _combine(x, route, g2, y, dest3, ts):
    n = x.shape[0]
    nt = n // ts
    row = lambda i: (i, 0)
    dest_spec = lambda f: pl.BlockSpec((1, 1, 2 * ts), f, memory_space=pltpu.SMEM)
    return pl.pallas_call(
        functools.partial(_combine_kernel, ts=ts),
        grid=(nt,),
        in_specs=[dest_spec(lambda i: (i, 0, 0)), dest_spec(lambda i: (jnp.minimum(i + 1, nt - 1), 0, 0)),
                  pl.BlockSpec((ts, D_MODEL), row), pl.BlockSpec((ts, LANES), row),
                  pl.BlockSpec((1, D_MODEL), lambda i: (0, 0)),
                  pl.BlockSpec(memory_space=pl.ANY)],
        out_specs=pl.BlockSpec((ts, D_MODEL), row),
        out_shape=jax.ShapeDtypeStruct((n, D_MODEL), F32),
        scratch_shapes=[pltpu.VMEM((2, 2, ts * SUBLANES, LANES), F32), pltpu.SemaphoreType.DMA((2,))],
        compiler_params=_cparams(1, 48),
        name="moe_combine",
    )(dest3, dest3, x, route, g2, y)


def _moe(streams, counts, layer, wg, wu, wd):
    experts = jnp.arange(N_EXPERTS, dtype=jnp.int32)
    cnt = counts[:, 0].astype(jnp.int32)
    pcounts = (cnt + MOE_BLOCK - 1) // MOE_BLOCK * MOE_BLOCK
    pend = jnp.cumsum(pcounts)
    pstart = pend - pcounts
    n_tok = sum(s[0].shape[0] for s in streams)
    n_blocks = -(-2 * n_tok // MOE_BLOCK) + N_EXPERTS
    blk_pos = jnp.arange(n_blocks, dtype=jnp.int32) * MOE_BLOCK
    blk_e = jnp.sum((pend[None, :] <= blk_pos[:, None]).astype(jnp.int32), axis=1)
    used = blk_pos < pend[-1]
    blk_e = jnp.where(used, blk_e, jnp.max(jnp.where(used, blk_e, 0)))
    n_used = (pend[-1:] // MOE_BLOCK).astype(jnp.int32)
    dests = []
    for x, h2, route, g2, ts in streams:
        e = route[:, 0:2].astype(jnp.int32)
        rank = route[:, 4:6].astype(jnp.int32)
        slot = jnp.sum(jnp.where(e[:, :, None] == experts, pstart, 0), axis=-1) + rank
        dests.append(slot * SUBLANES)
    buf = _dispatch([s[1] for s in streams], dests, [s[4] for s in streams], pend.astype(jnp.int32),
                    n_blocks * MOE_BLOCK)
    y = _experts(buf, blk_e, n_used, layer, wg, wu, wd)
    return [_combine(x, route, g2, y, dest.reshape(x.shape[0] // ts, 1, 2 * ts), ts)
            for (x, _, route, g2, ts), dest in zip(streams, dests)]


def _block_diag_ones(width):
    idx = np.arange(width) // HALF
    return jnp.asarray((idx[:, None] == idx[None, :]).astype(np.float32), dtype=BF16)


def _rope_tables(n_tok):
    t = jnp.arange(n_tok, dtype=jnp.int32)
    pos = jnp.stack([t // GRID_W, t % GRID_W], axis=-1).astype(F32)
    inv = ROPE_THETA ** (-jnp.arange(ROPE_FREQS, dtype=F32) / ROPE_FREQS)
    ang = pos[:, :, None] * inv
    cos, sin = jnp.cos(ang), jnp.sin(ang)
    c = jnp.concatenate([cos, cos], axis=-1).reshape(n_tok, HEAD_DIM)
    s = jnp.concatenate([-sin, sin], axis=-1).reshape(n_tok, HEAD_DIM)
    return jnp.tile(c, (1, 2)), jnp.tile(s, (1, 2))


def _tile_vec(v, reps):
    return jnp.tile(v.astype(F32), reps).reshape(1, -1)


def kernel(x, c, ctx, c_ctx, router_w, router_b, ada_w, ada_b, norm1_g, norm2_g, ev_w_in, ev_w_out, a_q_norm,
           a_k_norm, b_v_norm, b_ws, b_bs, od_w_in, od_w_out, c_q_norm, c_k_norm, c_rpb, d_dw, d_dw_b, d_ln_g,
           d_ln_b, moe_w_gate, moe_w_up, moe_w_down):
    assert x.shape[0] == 1 and ctx.shape[0] == 1
    s_len, n_ctx = x.shape[1], ctx.shape[1]
    ts_lat = min(512, s_len)
    ts_ctx = n_ctx
    x_lat = x[0]
    x_ctx = ctx[0]

    cc = jnp.zeros((8, D_MODEL), F32).at[0].set(c[0]).at[1].set(c_ctx)
    mods = _modulation(cc, ada_w, ada_b)

    def mod(l, row, k):
        return mods[l, row, k * D_MODEL:(k + 1) * D_MODEL].reshape(1, D_MODEL)

    bd512 = _block_diag_ones(A_Q)
    bd128 = _block_diag_ones(LANES)
    cos_lat, sin_lat = _rope_tables(s_len)
    cos_ctx, sin_ctx = jnp.ones((n_ctx, LANES), F32), jnp.zeros((n_ctx, LANES), F32)
    e_of_row = np.arange(N_EXPERTS)
    row_of_e = (e_of_row % EXPERTS_PER_GROUP) * SUBLANES + e_of_row // EXPERTS_PER_GROUP
    rw = jnp.zeros((LANES, D_MODEL), F32).at[row_of_e].set(router_w.T)
    rwh = rw.astype(BF16)
    rwl = (rw - rwh.astype(F32)).astype(BF16)
    rbc = jnp.zeros((LANES, 1), F32).at[row_of_e, 0].set(router_b)
    router = (rwh, rwl, rbc)

    for l in range(DEPTH):
        need_ctx = l < DEPTH - 1
        i = l // 2
        g_n1 = norm1_g[l].reshape(1, D_MODEL)
        g_n2 = norm2_g[l].reshape(1, D_MODEL)
        m_lat = [mod(l, 0, k) for k in range(6)]
        m_ctx = [mod(l, 1, k) for k in range(6)]
        if l % 2 == 0:
            w_in = ev_w_in[i].astype(BF16)
            w_out = ev_w_out[i].astype(BF16)
            consts = (bd512, bd128, _tile_vec(a_q_norm[i], A_HEADS), _tile_vec(a_k_norm[i], A_KV_HEADS),
                      b_v_norm[i].reshape(1, B_WIDTH).astype(F32), b_ws[i].astype(BF16).reshape(B_GROUPS // 2, 2 * CHUNK, CHUNK),
                      jnp.repeat(b_bs[i].T, HALF, axis=1))
            q_l, kd_l, v_l, b_l = _proj_even(x_lat, (g_n1, m_lat[1], m_lat[0]), w_in, consts, cos_lat, sin_lat, ts_lat)
            q_c, kd_c, v_c, b_c = _proj_even(x_ctx, (g_n1, m_ctx[1], m_ctx[0]), w_in, consts, cos_ctx, sin_ctx, ts_ctx)
            kd_cc = jnp.concatenate([kd_c, kd_c], axis=2)
            v_cc = jnp.concatenate([v_c, jnp.zeros_like(v_c)], axis=1)
            kd_all = jnp.concatenate([kd_l, kd_cc], axis=2)
            v_all = jnp.concatenate([v_l, v_cc], axis=1)
            mix_l = (_attn_even(q_l, kd_all, v_all, min(ATTN_TQ, s_len), ATTN_TK), b_l)
            if need_ctx:
                mix_c = (_attn_even(q_c, kd_cc, v_cc, min(ATTN_TQ, n_ctx), ATTN_TK), b_c)
        else:
            w_in = od_w_in[i].astype(BF16)
            w_out = od_w_out[i].astype(BF16)
            consts = (bd512, _tile_vec(c_q_norm[i], C_HEADS), _tile_vec(c_k_norm[i], C_HEADS))
            q_l, k_l, v_l, y_l = _proj_odd(x_lat, (g_n1, m_lat[1], m_lat[0]), w_in, consts, ts_lat)
            q_c, k_c, v_c, y_c = _proj_odd(x_ctx, (g_n1, m_ctx[1], m_ctx[0]), w_in, consts, ts_ctx)
            dw = jnp.zeros((CONV_W + 1, D_CH), F32).at[:CONV_W].set(d_dw[i])
            conv_p = (dw, d_dw_b[i].reshape(1, D_CH), d_ln_g[i].reshape(1, D_CH), d_ln_b[i].reshape(1, D_CH))
            mix_l = (_na_attention(q_l, k_l, v_l, k_c, v_c, _na_bias(c_rpb[i])), _conv_module(y_l, *conv_p, ts_lat))
            if need_ctx:
                mix_c = (_mha_small(q_c, k_c, v_c), _conv_module(y_c, *conv_p, ts_ctx))
        xn, h2, route, cnt = _post_mix(x_lat, *mix_l, w_out, (m_lat[2], g_n2, m_lat[4], m_lat[3]), router,
                                       jnp.zeros((N_EXPERTS, LANES), F32), ts_lat)
        streams = [(xn, h2, route, m_lat[5], ts_lat)]
        if need_ctx:
            xn, h2, route, cnt = _post_mix(x_ctx, *mix_c, w_out, (m_ctx[2], g_n2, m_ctx[4], m_ctx[3]), router,
                                           cnt, ts_ctx)
            streams.append((xn, h2, route, m_ctx[5], ts_ctx))
        outs = _moe(streams, cnt, l, moe_w_gate, moe_w_up, moe_w_down)
        x_lat = outs[0]
        if need_ctx:
            x_ctx = outs[1]
    return x_lat[None]
```

```python
import functools

import numpy as np
import jax
import jax.numpy as jnp
from jax import lax
from jax.experimental import pallas as pl
from jax.experimental.pallas import tpu as pltpu

F32 = jnp.float32
BF16 = jnp.bfloat16

D_MODEL = 1024
DEPTH = 4
GRID_W = 64
HEAD_DIM = 64
EPS = 1e-6
A_HEADS = 8
A_KV_HEADS = 2
A_Q = A_HEADS * HEAD_DIM
A_KV = A_KV_HEADS * HEAD_DIM
A_IN = A_Q + 2 * A_KV
ROPE_THETA = 10000.0
ROPE_FREQS = HEAD_DIM // 4
B_GROUPS = 8
B_WIDTH = 512
CHUNK = 128
C_HEADS = 8
C_W = C_HEADS * HEAD_DIM
NA_ROWS = 8
NA_COLS = 16
D_CH = 512
CONV_W = 31
EVEN_IN = A_IN + 2 * B_WIDTH
ODD_IN = 3 * C_W + 2 * D_CH
N_EXPERTS = 16
N_GROUPS = 4
EXPERTS_PER_GROUP = 4
D_EXPERT = 512
MOE_BLOCK = 512

LANES = 128
SUBLANES = 8
MXU_WIDTH = 256
HALF = 64
NEG_BIG = -1e30
CONV_HALO = 16
DMA_UNROLL = 8
ATTN_TQ = 512
ATTN_TK = 512
QK_SCALE_LOG2 =HEAD_DIM ** -0.5 * 1.4426950408889634


def _cparams(n_axes, vmem_mb):
    return pltpu.CompilerParams(dimension_semantics=("arbitrary",) * n_axes,
                                vmem_limit_bytes=vmem_mb << 20)


def _full(shape):
    nd = len(shape)
    return pl.BlockSpec(shape, lambda *_: (0,) * nd)


def _resident(shape):
    nd = len(shape)
    return pl.BlockSpec(shape, lambda *_: (0,) * nd, pipeline_mode=pl.Buffered(1))


def _dot(a, b):
    return jnp.dot(a, b, preferred_element_type=F32)


def _dot_nt(a, b):
    return lax.dot_general(a, b, (((1,), (1,)), ((), ())), preferred_element_type=F32)


def _split_bf16(x):
    hi = x.astype(BF16)
    lo = (x - hi.astype(F32)).astype(BF16)
    return hi, lo


def _seg_mean_sq(x, bd):
    hi, lo = _split_bf16(x * x)
    group = min(MXU_WIDTH, x.shape[1])
    bdg = bd[:group, :group]
    parts = [_dot(hi[:, c:c + group], bdg) + _dot(lo[:, c:c + group], bdg) for c in range(0, x.shape[1], group)]
    return (parts[0] if len(parts) == 1 else jnp.concatenate(parts, axis=1)) * (1.0 / HALF)


def _rms_mod(x, g, sc, sh):
    ms = jnp.mean(x * x, axis=-1, keepdims=True)
    return x * lax.rsqrt(ms + EPS) * g * (1.0 + sc) + sh


def _rope(x, c, sg, lane):
    sw = jnp.where((lane % 32) < 16, pltpu.roll(x, LANES - 16, 1), pltpu.roll(x, 16, 1))
    return x * c + sw * sg


def _pair_lhs(qs, lane_row):
    ma = jnp.where(lane_row < HALF, 1.0, 0.0).astype(BF16)
    mb = jnp.where(lane_row < HALF, 0.0, 1.0).astype(BF16)
    return jnp.concatenate([qs * ma, qs * mb], axis=0)


def _store_row_tiles(ref, val, t, row0=0):
    for k in range(SUBLANES):
        ref[pl.ds(row0 * SUBLANES + k, t, stride=SUBLANES), :] = val[:, k * LANES:(k + 1) * LANES]


def _load_row_tiles(ref, t, row0=0):
    return jnp.concatenate([ref[pl.ds(row0 * SUBLANES + k, t, stride=SUBLANES), :] for k in range(SUBLANES)], axis=1)


def _mod_kernel(cc_ref, w_ref, b_ref, o_ref):
    cc = cc_ref[...]
    s = cc * jax.nn.sigmoid(cc)
    sh, sl = _split_bf16(s)
    wh, wl = _split_bf16(w_ref[0])
    o_ref[0] = _dot(sh, wh) + _dot(sl, wh) + _dot(sh, wl) + b_ref[0]


def _modulation(cc, ada_w, ada_b):
    tn = 1024
    return pl.pallas_call(
        _mod_kernel,
        grid=(DEPTH, 6 * D_MODEL // tn),
        in_specs=[pl.BlockSpec((8, D_MODEL), lambda l, j: (0, 0)),
                  pl.BlockSpec((1, D_MODEL, tn), lambda l, j: (l, 0, j)),
                  pl.BlockSpec((1, 1, tn), lambda l, j: (l, 0, j))],
        out_specs=pl.BlockSpec((1, 8, tn), lambda l, j: (l, 0, j)),
        out_shape=jax.ShapeDtypeStruct((DEPTH, 8, 6 * D_MODEL), F32),
        compiler_params=_cparams(2, 32),
        name="modulation",
    )(cc, ada_w, ada_b.reshape(DEPTH, 1, 6 * D_MODEL))


def _proj_even_kernel(x_ref, g_ref, sc_ref, sh_ref, w_ref, bd512_ref, bd128_ref, qg_ref, kg_ref, vg_ref,
                      cos_ref, sin_ref, ws_ref, bsb_ref, q_ref, kd_ref, v_ref, b_ref, *, ts):
    h = _rms_mod(x_ref[...], g_ref[...], sc_ref[...], sh_ref[...])
    p = _dot(h.astype(BF16), w_ref[...])
    lane = lax.broadcasted_iota(jnp.int32, (ts, LANES), 1)
    c = cos_ref[...]
    sg = sin_ref[...]

    q = p[:, :A_Q]
    qn = q * lax.rsqrt(_seg_mean_sq(q, bd512_ref[...]) + EPS) * qg_ref[...]
    for sl in range(A_Q // LANES):
        xs = qn[:, sl * LANES:(sl + 1) * LANES]
        q_ref[:, sl * LANES:(sl + 1) * LANES] = (_rope(xs, c, sg, lane) * QK_SCALE_LOG2).astype(BF16)

    k = p[:, A_Q:A_Q + A_KV]
    kn = k * lax.rsqrt(_seg_mean_sq(k, bd128_ref[...]) + EPS) * kg_ref[...]
    kt = _rope(kn, c, sg, lane).T
    for j in range(A_KV_HEADS):
        kj = kt[j * HALF:(j + 1) * HALF]
        kd_ref[j] = jnp.concatenate([kj, kj], axis=0).astype(BF16)
    v = p[:, A_Q + A_KV:A_IN]
    v_ref[0] = jnp.where(lane < HALF, v, 1.0).astype(BF16)
    v_ref[1] = jnp.where(lane < HALF, pltpu.roll(v, HALF, 1), 1.0).astype(BF16)

    zu = jax.nn.gelu(p[:, A_IN:A_IN + B_WIDTH])
    zv = jax.nn.gelu(p[:, A_IN + B_WIDTH:])
    vn = (zv * lax.rsqrt(_seg_mean_sq(zv, bd512_ref[...]) + EPS) * vg_ref[...]).astype(BF16)
    lane_c = lax.broadcasted_iota(jnp.int32, (CHUNK, LANES), 1)
    for ch in range(ts // CHUNK):
        rows = slice(ch * CHUNK, (ch + 1) * CHUNK)
        for sl in range(B_WIDTH // LANES):
            cols = slice(sl * LANES, (sl + 1) * LANES)
            vs = vn[rows, cols]
            both = _dot(ws_ref[sl], vs)
            mixed = jnp.where(lane_c < HALF, both[:CHUNK], both[CHUNK:])
            b_ref[rows, cols] = (zu[rows, cols] * (mixed + bsb_ref[:, cols])).astype(BF16)


def _proj_even(x, mods, w_in, consts, cos, sin, ts):
    n = x.shape[0]
    g, sc, sh = mods
    bd512, bd128, qg, kg, vg, ws, bsb = consts
    row = lambda i: (i, 0)
    vec = lambda w: pl.BlockSpec((1, w), lambda i: (0, 0))
    return pl.pallas_call(
        functools.partial(_proj_even_kernel, ts=ts),
        grid=(n // ts,),
        in_specs=[pl.BlockSpec((ts, D_MODEL), row), vec(D_MODEL), vec(D_MODEL), vec(D_MODEL),
                  _full((D_MODEL, EVEN_IN)), _full((A_Q, A_Q)), _full((LANES, LANES)),
                  vec(A_Q), vec(A_KV), vec(B_WIDTH),
                  pl.BlockSpec((ts, LANES), row), pl.BlockSpec((ts, LANES), row),
                  _full((B_GROUPS // 2, 2 * CHUNK, CHUNK)), _full((CHUNK, B_WIDTH))],
        out_specs=[pl.BlockSpec((ts, A_Q), row),
                   pl.BlockSpec((A_KV_HEADS, LANES, ts), lambda i: (0, 0, i)),
                   pl.BlockSpec((A_KV_HEADS, ts, LANES), lambda i: (0, i, 0)),
                   pl.BlockSpec((ts, B_WIDTH), row)],
        out_shape=[jax.ShapeDtypeStruct((n, A_Q), BF16),
                   jax.ShapeDtypeStruct((A_KV_HEADS, LANES, n), BF16),
                   jax.ShapeDtypeStruct((A_KV_HEADS, n, LANES), BF16),
                   jax.ShapeDtypeStruct((n, B_WIDTH), BF16)],
        compiler_params=_cparams(1, 56),
        name="proj_even",
    )(x, g, sc, sh, w_in, bd512, bd128, qg, kg, vg, cos, sin, ws, bsb)


def _attn_even_kernel(q_ref, kd_ref, va_ref, o_ref, lhs_scr, s_a, s_b, mx_a, mx_b, m_scr, acc_scr,
                      *, tq, tk, n_chunks):
    rows = 4 * tq
    lane_row = lax.broadcasted_iota(jnp.int32, (1, LANES), 1)
    lane = lax.broadcasted_iota(jnp.int32, (tq, LANES), 1)
    for j in range(A_KV_HEADS):
        lhs_scr[...] = jnp.concatenate([_pair_lhs(q_ref[:, sl * LANES:(sl + 1) * LANES], lane_row)
                                        for sl in (2 * j, 2 * j + 1)], axis=0)
        m_scr[...] = jnp.full((rows, LANES), NEG_BIG, F32)
        acc_scr[...] = jnp.zeros((rows, LANES), F32)

        def scores(c, s_scr, mx_scr, j=j):
            off = pl.multiple_of(c * tk, tk)
            s = _dot(lhs_scr[...], kd_ref[j, :, pl.ds(off, tk)])
            s_scr[...] = s
            mx_scr[...] = jnp.broadcast_to(jnp.max(s, axis=-1, keepdims=True), (rows, LANES))

        def accumulate(c, s_scr, mx_scr, j=j):
            off = pl.multiple_of(c * tk, tk)
            m_old = m_scr[...]
            m_new = jnp.maximum(m_old, mx_scr[...])
            p = jnp.concatenate([jnp.exp2(s_scr[:, t * LANES:(t + 1) * LANES] - m_new)
                                 for t in range(tk // LANES)], axis=1).astype(BF16)
            pv = _dot(p, va_ref[j, pl.ds(off, tk), :])
            acc_scr[...] = jnp.exp2(m_old - m_new) * acc_scr[...] + pv
            m_scr[...] = m_new

        scores(0, s_a, mx_a)

        def body(i, carry):
            scores(2 * i + 1, s_b, mx_b)
            accumulate(2 * i, s_a, mx_a)
            scores(2 * i + 2, s_a, mx_a)
            accumulate(2 * i + 1, s_b, mx_b)
            return carry

        lax.fori_loop(0, (n_chunks - 1) // 2, body, 0, unroll=2)
        accumulate(n_chunks - 1, s_a, mx_a)
        acc = acc_scr[...]
        o = acc / pltpu.roll(acc, HALF, 1)
        for t, sl in enumerate((2 * j, 2 * j + 1)):
            oa = o[(2 * t) * tq:(2 * t + 1) * tq]
            ob = o[(2 * t + 1) * tq:(2 * t + 2) * tq]
            o_ref[:, sl * LANES:(sl + 1) * LANES] = jnp.where(lane < HALF, oa, pltpu.roll(ob, HALF, 1)).astype(BF16)


def _attn_even(q, kd, va, tq, tk):
    n = q.shape[0]
    n_chunks = kd.shape[2] // tk
    assert n_chunks % 2 == 1
    rows = 4 * tq
    big = lambda w: pltpu.VMEM((rows, w), F32)
    return pl.pallas_call(
        functools.partial(_attn_even_kernel, tq=tq, tk=tk, n_chunks=n_chunks),
        grid=(n // tq,),
        in_specs=[pl.BlockSpec((tq, A_Q), lambda i: (i, 0)), _resident(kd.shape), _resident(va.shape)],
        out_specs=pl.BlockSpec((tq, A_Q), lambda i: (i, 0)),
        out_shape=jax.ShapeDtypeStruct((n, A_Q), BF16),
        scratch_shapes=[pltpu.VMEM((rows, LANES), BF16), big(tk), big(tk), big(LANES), big(LANES),
                        big(LANES), big(LANES)],
        compiler_params=_cparams(1, 56),
        name="attn_even",
    )(q, kd, va)


def _proj_odd_kernel(x_ref, g_ref, sc_ref, sh_ref, w_ref, bd512_ref, qg_ref, kg_ref,
                     q_ref, k_ref, v_ref, y_ref):
    h = _rms_mod(x_ref[...], g_ref[...], sc_ref[...], sh_ref[...])
    p = _dot(h.astype(BF16), w_ref[...])
    bd = bd512_ref[...]
    q = p[:, :C_W]
    q_ref[...] = (q * lax.rsqrt(_seg_mean_sq(q, bd) + EPS) * qg_ref[...] * (HEAD_DIM ** -0.5)).astype(BF16)
    k = p[:, C_W:2 * C_W]
    k_ref[...] = (k * lax.rsqrt(_seg_mean_sq(k, bd) + EPS) * kg_ref[...]).astype(BF16)
    v_ref[...] = p[:, 2 * C_W:3 * C_W].astype(BF16)
    y_ref[...] = p[:, 3 * C_W:3 * C_W + D_CH] * jax.nn.sigmoid(p[:, 3 * C_W + D_CH:])


def _proj_odd(x, mods, w_in, consts, ts):
    n = x.shape[0]
    g, sc, sh = mods
    bd512, qg, kg = consts
    row = lambda i: (i, 0)
    vec = lambda w: pl.BlockSpec((1, w), lambda i: (0, 0))
    blk = pl.BlockSpec((ts, C_W), row)
    return pl.pallas_call(
        _proj_odd_kernel,
        grid=(n // ts,),
        in_specs=[pl.BlockSpec((ts, D_MODEL), row), vec(D_MODEL), vec(D_MODEL), vec(D_MODEL),
                  _full((D_MODEL, ODD_IN)), _full((C_W, C_W)), vec(C_W), vec(C_W)],
        out_specs=[blk, blk, blk, blk],
        out_shape=[jax.ShapeDtypeStruct((n, C_W), BF16), jax.ShapeDtypeStruct((n, C_W), BF16),
                   jax.ShapeDtypeStruct((n, C_W), BF16), jax.ShapeDtypeStruct((n, D_CH), F32)],
        compiler_params=_cparams(1, 56),
        name="proj_odd",
    )(x, g, sc, sh, w_in, bd512, qg, kg)


NA_BLOCK_ROWS = 8
NA_TOK = NA_BLOCK_ROWS * GRID_W
NA_WIN = NA_ROWS * GRID_W


def _na_kernel(q_ref, kp_ref, kc_ref, kn_ref, vp_ref, vc_ref, vn_ref, kx_ref, vx_ref, bias_ref, o_ref,
               kbuf, vbuf, *, rows):
    i = pl.program_id(0)
    for t, (kr, vr) in enumerate(((kp_ref, vp_ref), (kc_ref, vc_ref), (kn_ref, vn_ref))):
        kbuf[t * NA_TOK:(t + 1) * NA_TOK] = kr[...]
        vbuf[t * NA_TOK:(t + 1) * NA_TOK] = vr[...]
    lane_row = lax.broadcasted_iota(jnp.int32, (1, LANES), 1)
    lane = lax.broadcasted_iota(jnp.int32, (GRID_W, LANES), 1)
    for sl in range(C_W // LANES):
        cols = slice(sl * LANES, (sl + 1) * LANES)
        lhs = _pair_lhs(q_ref[:, cols], lane_row)
        s_ctx = _dot_nt(lhs, kx_ref[:, cols])
        sa, sb, starts = [], [], []
        for j in range(NA_BLOCK_ROWS):
            r = i * NA_BLOCK_ROWS + j
            r0 = jnp.clip(r - NA_ROWS // 2, 0, rows - NA_ROWS)
            start = pl.multiple_of((r0 - (i - 1) * NA_BLOCK_ROWS) * GRID_W, GRID_W)
            starts.append(start)
            qa = lhs[j * GRID_W:(j + 1) * GRID_W]
            qb = lhs[NA_TOK + j * GRID_W:NA_TOK + (j + 1) * GRID_W]
            kw = kbuf[pl.ds(start, NA_WIN), cols]
            s = _dot_nt(jnp.concatenate([qa, qb], axis=0), kw) + bias_ref[r0 - r + NA_ROWS - 1, sl]
            sa.append(s[:GRID_W])
            sb.append(s[GRID_W:])
        s_loc = jnp.concatenate(sa + sb, axis=0)
        m = jnp.maximum(jnp.max(s_loc, axis=-1, keepdims=True), jnp.max(s_ctx, axis=-1, keepdims=True))
        p_loc = jnp.exp(s_loc - m)
        p_ctx = jnp.exp(s_ctx - m)
        l = jnp.sum(p_loc, axis=-1, keepdims=True) + jnp.sum(p_ctx, axis=-1, keepdims=True)
        o_ctx = _dot(p_ctx.astype(BF16), vx_ref[:, cols])
        p_loc = p_loc.astype(BF16)
        for j in range(NA_BLOCK_ROWS):
            ra = slice(j * GRID_W, (j + 1) * GRID_W)
            rb = slice(NA_TOK + j * GRID_W, NA_TOK + (j + 1) * GRID_W)
            vw = vbuf[pl.ds(starts[j], NA_WIN), cols]
            o2 = _dot(jnp.concatenate([p_loc[ra], p_loc[rb]], axis=0), vw)
            oa = (o2[:GRID_W] + o_ctx[ra]) / l[ra]
            ob = (o2[GRID_W:] + o_ctx[rb]) / l[rb]
            o_ref[ra, cols] = jnp.where(lane < HALF, oa, ob).astype(BF16)


def _na_attention(q, k, v, kx, vx, bias):
    n = q.shape[0]
    nb = n // NA_TOK
    blk = lambda f: pl.BlockSpec((NA_TOK, C_W), f)
    prev = lambda i: (jnp.maximum(i - 1, 0), 0)
    cur = lambda i: (i, 0)
    nxt = lambda i: (jnp.minimum(i + 1, nb - 1), 0)
    return pl.pallas_call(
        functools.partial(_na_kernel, rows=n // GRID_W),
        grid=(nb,),
        in_specs=[blk(cur), blk(prev), blk(cur), blk(nxt), blk(prev), blk(cur), blk(nxt),
                  _full(kx.shape), _full(vx.shape), _full(bias.shape)],
        out_specs=blk(cur),
        out_shape=jax.ShapeDtypeStruct((n, C_W), BF16),
        scratch_shapes=[pltpu.VMEM((3 * NA_TOK, C_W), BF16), pltpu.VMEM((3 * NA_TOK, C_W), BF16)],
        compiler_params=_cparams(1, 56),
        name="na_attention",
    )(q, k, k, k, v, v, v, kx, vx, bias)


def _na_bias(rpb):
    cc = np.arange(GRID_W)
    c0 = np.clip(cc - NA_COLS // 2, 0, GRID_W - NA_COLS)
    kc = np.arange(GRID_W)
    inwin = (kc[None, :] >= c0[:, None]) & (kc[None, :] < c0[:, None] + NA_COLS)
    dc = np.clip(kc[None, :] - cc[:, None] + NA_COLS - 1, 0, 2 * NA_COLS - 2)
    n_dr, n_dc = 2 * NA_ROWS - 1, 2 * NA_COLS - 1
    sel = np.zeros((n_dc, GRID_W * GRID_W), np.float32)
    sel[dc.ravel(), np.arange(GRID_W * GRID_W)] = inwin.ravel()
    t2 = jnp.dot(rpb.astype(F32).reshape(C_HEADS * n_dr, n_dc), jnp.asarray(sel), precision=lax.Precision.HIGHEST)
    t2 = t2.reshape(C_HEADS, n_dr, GRID_W, GRID_W) + jnp.asarray(np.where(inwin, 0.0, NEG_BIG), F32)
    t = jnp.stack([t2[:, oi:oi + NA_ROWS] for oi in range(NA_ROWS)], axis=0)
    t = jnp.transpose(t, (0, 1, 3, 2, 4))
    return t.reshape(NA_ROWS, C_HEADS // 2, 2 * GRID_W, NA_WIN)


def _mha_kernel(q_ref, k_ref, v_ref, o_ref):
    n = q_ref.shape[0]
    lane_row = lax.broadcasted_iota(jnp.int32, (1, LANES), 1)
    lane = lax.broadcasted_iota(jnp.int32, (n, LANES), 1)
    for sl in range(C_W // LANES):
        cols = slice(sl * LANES, (sl + 1) * LANES)
        s = _dot_nt(_pair_lhs(q_ref[:, cols], lane_row), k_ref[:, cols])
        p = jnp.exp(s - jnp.max(s, axis=-1, keepdims=True))
        l = jnp.sum(p, axis=-1, keepdims=True)
        o = _dot(p.astype(BF16), v_ref[:, cols]) / l
        o_ref[:, cols] = jnp.where(lane < HALF, o[:n], o[n:]).astype(BF16)


def _mha_small(q, k, v):
    return pl.pallas_call(
        _mha_kernel,
        grid=(1,),
        in_specs=[_full(q.shape), _full(k.shape), _full(v.shape)],
        out_specs=_full(q.shape),
        out_shape=jax.ShapeDtypeStruct(q.shape, BF16),
        compiler_params=_cparams(1, 32),
        name="mha_small",
    )(q, k, v)


def _conv_kernel(y_ref, yp_ref, yn_ref, dw_ref, dwb_ref, lng_ref, lnb_ref, o_ref, ext, shifted, *, ts):
    i = pl.program_id(0)
    nb = pl.num_programs(0)
    ext[0:CONV_HALO] = jnp.where(i > 0, yp_ref[...], 0.0)
    ext[CONV_HALO:CONV_HALO + ts] = y_ref[...]
    ext[CONV_HALO + ts:] = jnp.where(i < nb - 1, yn_ref[...], 0.0)
    base = CONV_HALO - CONV_W // 2
    span = ts + (base + CONV_W - 1) // SUBLANES * SUBLANES
    for s in range(SUBLANES):
        shifted[s] = ext[s:s + span, :]
    acc = jnp.zeros((ts, D_CH), F32)
    for j in range(CONV_W):
        s, start = (base + j) % SUBLANES, (base + j) // SUBLANES * SUBLANES
        acc = acc + shifted[s, start:start + ts, :] * dw_ref[j:j + 1, :]
    yb = acc + dwb_ref[...]
    mu = jnp.mean(yb, axis=-1, keepdims=True)
    var = jnp.mean(jnp.square(yb - mu), axis=-1, keepdims=True)
    z = (yb - mu) * lax.rsqrt(var + EPS) * lng_ref[...] + lnb_ref[...]
    o_ref[...] = (z * jax.nn.sigmoid(z)).astype(BF16)


def _conv_module(y, dw, dwb, lng, lnb, ts):
    n = y.shape[0]
    hb = ts // CONV_HALO
    nh = n // CONV_HALO
    vec = pl.BlockSpec((1, D_CH), lambda i: (0, 0))
    return pl.pallas_call(
        functools.partial(_conv_kernel, ts=ts),
        grid=(n // ts,),
        in_specs=[pl.BlockSpec((ts, D_CH), lambda i: (i, 0)),
                  pl.BlockSpec((CONV_HALO, D_CH), lambda i: (jnp.maximum(i * hb - 1, 0), 0)),
                  pl.BlockSpec((CONV_HALO, D_CH), lambda i: (jnp.minimum((i + 1) * hb, nh - 1), 0)),
                  _full((CONV_W + 1, D_CH)), vec, vec, vec],
        out_specs=pl.BlockSpec((ts, D_CH), lambda i: (i, 0)),
        out_shape=jax.ShapeDtypeStruct((n, D_CH), BF16),
        scratch_shapes=[pltpu.VMEM((ts + 2 * CONV_HALO, D_CH), F32),
                        pltpu.VMEM((SUBLANES, ts + 2 * CONV_HALO - SUBLANES, D_CH), F32)],
        compiler_params=_cparams(1, 48),
        name="conv_module",
    )(y, y, y, dw, dwb, lng, lnb)


def _first_max2(vals):
    def first_max(vs):
        m = functools.reduce(jnp.maximum, vs)
        idx = len(vs) - 1
        for k in range(len(vs) - 2, -1, -1):
            idx = jnp.where(vs[k] == m, k, idx)
        return m, idx

    m1, i1 = first_max(vals)
    m2, i2 = first_max([jnp.where(i1 == k, -jnp.inf, v) for k, v in enumerate(vals)])
    return m1, m2, i1, i2


def _pick(idx, vals):
    out = vals[-1]
    for k in range(len(vals) - 2, -1, -1):
        out = jnp.where(idx == k, vals[k], out)
    return out


def _route_t(lt, rbc, tri, carry, ts):
    nk = EXPERTS_PER_GROUP
    aff = jax.nn.sigmoid(lt[:nk * SUBLANES])
    sel = aff + rbc[:nk * SUBLANES]
    sel_k = [sel[k * SUBLANES:(k + 1) * SUBLANES] for k in range(nk)]
    aff_k = [aff[k * SUBLANES:(k + 1) * SUBLANES] for k in range(nk)]
    m1, m2, _, _ = _first_max2(sel_k)
    gscore = m1 + m2
    best, bg = gscore[0:1], jnp.zeros((1, ts), jnp.int32)
    for g in range(1, N_GROUPS):
        upd = gscore[g:g + 1] > best
        best = jnp.where(upd, gscore[g:g + 1], best)
        bg = jnp.where(upd, g, bg)
    sel_b = [_pick(bg, [v[g:g + 1] for g in range(N_GROUPS)]) for v in sel_k]
    aff_b = [_pick(bg, [v[g:g + 1] for g in range(N_GROUPS)]) for v in aff_k]
    _, _, i1, i2 = _first_max2(sel_b)
    a1, a2 = _pick(i1, aff_b), _pick(i2, aff_b)
    e1 = bg * nk + i1
    e2 = bg * nk + i2
    erow = lax.broadcasted_iota(jnp.int32, (N_EXPERTS, ts), 0)
    hit1 = erow == e1
    hit2 = erow == e2
    onehot = jnp.where(hit1 | hit2, 1.0, 0.0)
    before = _dot(onehot.astype(BF16), tri) + carry[:, 0:1]
    r1 = jnp.sum(jnp.where(hit1, before, 0.0), axis=0, keepdims=True)
    r2 = jnp.sum(jnp.where(hit2, before, 0.0), axis=0, keepdims=True)
    new_carry = carry + jnp.sum(onehot, axis=1, keepdims=True)
    tot = a1 + a2
    rows = [e1.astype(F32), e2.astype(F32), a1 / tot, a2 / tot, r1, r2, jnp.zeros((2, ts), F32)]
    return jnp.concatenate(rows, axis=0), new_carry


def _post_mix_kernel(x_ref, a_ref, b_ref, w_ref, g1_ref, ng_ref, sc_ref, sh_ref, rwh_ref, rwl_ref, rb_ref, tri_ref,
                     cnt0_ref, xo_ref, h2_ref, route_ref, cnt_ref, *, ts):
    @pl.when(pl.program_id(0) == 0)
    def _():
        cnt_ref[...] = cnt0_ref[...]

    half = w_ref.shape[0] // 2
    mo = _dot(a_ref[...], w_ref[:half]) + _dot(b_ref[...], w_ref[half:])
    xn = x_ref[...] + g1_ref[...] * mo
    xo_ref[...] = xn
    h2 = _rms_mod(xn, ng_ref[...], sc_ref[...], sh_ref[...])
    _store_row_tiles(h2_ref, h2, ts)
    hh, hl = _split_bf16(h2)
    lt = _dot_nt(rwh_ref[...], hh) + _dot_nt(rwh_ref[...], hl) + _dot_nt(rwl_ref[...], hh)
    rt, cnt = _route_t(lt, rb_ref[...], tri_ref[...], cnt_ref[...], ts)
    cnt_ref[...] = cnt
    route_ref[...] = jnp.concatenate([rt, jnp.zeros((LANES - SUBLANES, ts), F32)], axis=0).T


def _post_mix(x, a, b, w_out, mods, router, cnt0, ts):
    n = x.shape[0]
    g1, ng, sc, sh = mods
    rwh, rwl, rbc = router
    tri = jnp.asarray(np.triu(np.ones((ts, ts), np.float32), 1), dtype=BF16)
    row = lambda i: (i, 0)
    vec = lambda w: pl.BlockSpec((1, w), lambda i: (0, 0))
    return pl.pallas_call(
        functools.partial(_post_mix_kernel, ts=ts),
        grid=(n // ts,),
        in_specs=[pl.BlockSpec((ts, D_MODEL), row), pl.BlockSpec((ts, a.shape[1]), row),
                  pl.BlockSpec((ts, b.shape[1]), row), _full(w_out.shape),
                  vec(D_MODEL), vec(D_MODEL), vec(D_MODEL), vec(D_MODEL),
                  _full(rwh.shape), _full(rwl.shape), _full(rbc.shape), _full(tri.shape),
                  _full((N_EXPERTS, LANES))],
        out_specs=[pl.BlockSpec((ts, D_MODEL), row), pl.BlockSpec((ts * SUBLANES, LANES), row),
                   pl.BlockSpec((ts, LANES), row), _full((N_EXPERTS, LANES))],
        out_shape=[jax.ShapeDtypeStruct((n, D_MODEL), F32), jax.ShapeDtypeStruct((n * SUBLANES, LANES), F32),
                   jax.ShapeDtypeStruct((n, LANES), F32), jax.ShapeDtypeStruct((N_EXPERTS, LANES), F32)],
        compiler_params=_cparams(1, 56),
        name="post_mix",
    )(x, a, b, w_out, g1, ng, sc, sh, rwh, rwl, rbc, tri, cnt0)


def _start_tile_copies(make_copy, ts):
    def issue(i, carry):
        copies = [make_copy(i * DMA_UNROLL + u, k) for u in range(DMA_UNROLL) for k in range(2)]
        for j, cp in enumerate(copies):
            cp.start(priority=j % 2)
        return carry

    lax.fori_loop(0, ts // DMA_UNROLL, issue, 0)


def _wait_tile_copies(make_copy, ts):
    def drain(i, carry):
        for u in range(DMA_UNROLL):
            make_copy(i * DMA_UNROLL + u, 0).wait()
            make_copy(i * DMA_UNROLL + u, 1).wait()
        return carry

    lax.fori_loop(0, ts // DMA_UNROLL, drain, 0)


def _dispatch_kernel(pend_ref, dest_ref, *rest, tss, tile_offsets):
    n_streams = len(tile_offsets) - 1
    h_refs = rest[:n_streams]
    buf_ref, zeros_scr, zsem, sem = rest[n_streams:]
    i = pl.program_id(0)
    pl.when(i == 0)(functools.partial(_define_padding, pend_ref, buf_ref, zeros_scr, zsem))
    for s, h_ref in enumerate(h_refs):

        def row_copy(r, k, h_ref=h_ref):
            src = h_ref.at[pl.ds(pl.multiple_of(r * SUBLANES, SUBLANES), SUBLANES)]
            dst = buf_ref.at[pl.ds(pl.multiple_of(dest_ref[0, 0, 2 * r + k], SUBLANES), SUBLANES)]
            return pltpu.make_async_copy(src, dst, sem)

        @pl.when((i >= tile_offsets[s]) & (i < tile_offsets[s + 1]))
        def _(row_copy=row_copy, ts=tss[s]):
            _start_tile_copies(row_copy, ts)
            _wait_tile_copies(row_copy, ts)


def _define_padding(pend_ref, buf_ref, zeros_scr, zsem):
    zeros_scr[...] = jnp.zeros_like(zeros_scr)
    block_rows = MOE_BLOCK * SUBLANES

    def zero_block(first_row):
        return pltpu.make_async_copy(zeros_scr, buf_ref.at[pl.ds(pl.multiple_of(first_row, SUBLANES), block_rows)], zsem)

    def last_block(e):
        return zero_block((pend_ref[e] - MOE_BLOCK) * SUBLANES)

    def nonempty(e):
        return pend_ref[e] > (pend_ref[e - 1] if e > 0 else 0)

    for e in range(N_EXPERTS):
        pl.when(nonempty(e))(lambda e=e: last_block(e).start())
    for e in range(N_EXPERTS):
        pl.when(nonempty(e))(lambda e=e: last_block(e).wait())
    n_used = pend_ref[N_EXPERTS - 1] // MOE_BLOCK
    n_blocks = buf_ref.shape[0] // block_rows
    lax.fori_loop(n_used, n_blocks, lambda b, c: (zero_block(b * block_rows).start(), c)[1], 0)
    lax.fori_loop(n_used, n_blocks, lambda b, c: (zero_block(b * block_rows).wait(), c)[1], 0)


def _dispatch(h2ts, dests, tss, pend, n_rows):
    tiles = [h.shape[0] // (SUBLANES * ts) for h, ts in zip(h2ts, tss)]
    offs = [sum(tiles[:s]) for s in range(len(tiles) + 1)]
    width = 2 * max(tss)
    dest3 = jnp.concatenate([jnp.pad(d.reshape(t, 2 * ts), ((0, 0), (0, width - 2 * ts)))
                             for d, t, ts in zip(dests, tiles, tss)], axis=0).reshape(offs[-1], 1, width)
    in_specs = [pl.BlockSpec((1, 1, width), lambda i, pe: (i, 0, 0), memory_space=pltpu.SMEM)]
    for s in range(len(h2ts)):
        in_specs.append(pl.BlockSpec((tss[s] * SUBLANES, LANES),
                                     lambda i, pe, s=s: (jnp.clip(i - offs[s], 0, tiles[s] - 1), 0)))
    grid_spec = pltpu.PrefetchScalarGridSpec(
        num_scalar_prefetch=1, grid=(offs[-1],), in_specs=in_specs,
        out_specs=pl.BlockSpec(memory_space=pl.ANY),
        scratch_shapes=[pltpu.VMEM((MOE_BLOCK * SUBLANES, LANES), F32), pltpu.SemaphoreType.DMA(()),
                        pltpu.SemaphoreType.DMA(())])
    return pl.pallas_call(
        functools.partial(_dispatch_kernel, tss=tuple(tss), tile_offsets=tuple(offs)),
        grid_spec=grid_spec,
        out_shape=jax.ShapeDtypeStruct((n_rows * SUBLANES, LANES), F32),
        compiler_params=_cparams(1, 32),
        name="moe_dispatch",
    )(pend, dest3, *h2ts)


def _expert_kernel(blk_e_ref, n_used_ref, x_ref, wg_ref, wu_ref, wd_ref, y_ref, wg_s, wu_s, wd_s):
    i = pl.program_id(0)

    @pl.when((i == 0) | (blk_e_ref[i] != blk_e_ref[jnp.maximum(i - 1, 0)]))
    def _():
        wg_s[...] = wg_ref[0, 0].astype(BF16)
        wu_s[...] = wu_ref[0, 0].astype(BF16)
        wd_s[...] = wd_ref[0, 0].astype(BF16)

    @pl.when(i < n_used_ref[0])
    def _():
        xb = _load_row_tiles(x_ref, MOE_BLOCK).astype(BF16)
        gate = _dot(xb, wg_s[...])
        up = _dot(xb, wu_s[...])
        hmid = gate * jax.nn.sigmoid(gate) * up
        _store_row_tiles(y_ref, _dot(hmid.astype(BF16), wd_s[...]), MOE_BLOCK)

    @pl.when(i >= n_used_ref[0])
    def _():
        y_ref[...] = jnp.zeros_like(y_ref)


def _experts(buf, blk_e, n_used, layer, wg, wu, wd):
    nb = buf.shape[0] // (MOE_BLOCK * SUBLANES)
    used = lambda i, be, nu: (jnp.minimum(i, nu[0] - 1), 0)
    wsel = lambda i, be, nu: (layer, be[i], 0, 0)
    grid_spec = pltpu.PrefetchScalarGridSpec(
        num_scalar_prefetch=2,
        grid=(nb,),
        in_specs=[pl.BlockSpec((MOE_BLOCK * SUBLANES, LANES), used),
                  pl.BlockSpec((1, 1, D_MODEL, D_EXPERT), wsel),
                  pl.BlockSpec((1, 1, D_MODEL, D_EXPERT), wsel),
                  pl.BlockSpec((1, 1, D_EXPERT, D_MODEL), wsel)],
        out_specs=pl.BlockSpec((MOE_BLOCK * SUBLANES, LANES), lambda i, be, nu: (i, 0)),
        scratch_shapes=[pltpu.VMEM((D_MODEL, D_EXPERT), BF16), pltpu.VMEM((D_MODEL, D_EXPERT), BF16),
                        pltpu.VMEM((D_EXPERT, D_MODEL), BF16)],
    )
    return pl.pallas_call(
        _expert_kernel,
        grid_spec=grid_spec,
        out_shape=jax.ShapeDtypeStruct(buf.shape, F32),
        compiler_params=_cparams(1, 48),
        name="moe_experts",
    )(blk_e, n_used, buf, wg, wu, wd)


def _combine_kernel(dest_ref, dest_next_ref, x_ref, route_ref, g2_ref, y_ref, o_ref, gbuf, sems, *, ts):
    i = pl.program_id(0)
    nt = pl.num_programs(0)

    def gather(d_ref, slot):
        def row_copy(r, k):
            src = y_ref.at[pl.ds(pl.multiple_of(d_ref[0, 0, 2 * r + k], SUBLANES), SUBLANES)]
            dst = gbuf.at[slot, k, pl.ds(pl.multiple_of(r * SUBLANES, SUBLANES), SUBLANES)]
            return pltpu.make_async_copy(src, dst, sems.at[slot])
        return row_copy

    slot = i % 2
    pl.when(i == 0)(lambda: _start_tile_copies(gather(dest_ref, 0), ts))
    pl.when(i + 1 < nt)(lambda: _start_tile_copies(gather(dest_next_ref, 1 - slot), ts))
    _wait_tile_copies(gather(dest_ref, slot), ts)
    route = route_ref[...]
    w0, w1 = route[:, 2:3], route[:, 3:4]
    for k in range(SUBLANES):
        cols = slice(k * LANES, (k + 1) * LANES)
        f = (gbuf[slot, 0, pl.ds(k, ts, stride=SUBLANES), :] * w0
             + gbuf[slot, 1, pl.ds(k, ts, stride=SUBLANES), :] * w1)
        o_ref[:, cols] = x_ref[:, cols] + g2_ref[:, cols] * f


def _combine(x, route, g2, y, dest3, ts):
    n = x.shape[0]
    nt = n // ts
    row = lambda i: (i, 0)
    dest_spec = lambda f: pl.BlockSpec((1, 1, 2 * ts), f, memory_space=pltpu.SMEM)
    return pl.pallas_call(
        functools.partial(_combine_kernel, ts=ts),
        grid=(nt,),
        in_specs=[dest_spec(lambda i: (i, 0, 0)), dest_spec(lambda i: (jnp.minimum(i + 1, nt - 1), 0, 0)),
                  pl.BlockSpec((ts, D_MODEL), row), pl.BlockSpec((ts, LANES), row),
                  pl.BlockSpec((1, D_MODEL), lambda i: (0, 0)),
                  pl.BlockSpec(memory_space=pl.ANY)],
        out_specs=pl.BlockSpec((ts, D_MODEL), row),
        out_shape=jax.ShapeDtypeStruct((n, D_MODEL), F32),
        scratch_shapes=[pltpu.VMEM((2, 2, ts * SUBLANES, LANES), F32), pltpu.SemaphoreType.DMA((2,))],
        compiler_params=_cparams(1, 48),
        name="moe_combine",
    )(dest3, dest3, x, route, g2, y)


def _moe(streams, counts, layer, wg, wu, wd):
    experts = jnp.arange(N_EXPERTS, dtype=jnp.int32)
    cnt = counts[:, 0].astype(jnp.int32)
    pcounts = (cnt + MOE_BLOCK - 1) // MOE_BLOCK * MOE_BLOCK
    pend = jnp.cumsum(pcounts)
    pstart = pend - pcounts
    n_tok = sum(s[0].shape[0] for s in streams)
    n_blocks = -(-2 * n_tok // MOE_BLOCK) + N_EXPERTS
    blk_pos = jnp.arange(n_blocks, dtype=jnp.int32) * MOE_BLOCK
    blk_e = jnp.sum((pend[None, :] <= blk_pos[:, None]).astype(jnp.int32), axis=1)
    used = blk_pos < pend[-1]
    blk_e = jnp.where(used, blk_e, jnp.max(jnp.where(used, blk_e, 0)))
    n_used = (pend[-1:] // MOE_BLOCK).astype(jnp.int32)
    dests = []
    for x, h2, route, g2, ts in streams:
        e = route[:, 0:2].astype(jnp.int32)
        rank = route[:, 4:6].astype(jnp.int32)
        slot = jnp.sum(jnp.where(e[:, :, None] == experts, pstart, 0), axis=-1) + rank
        dests.append(slot * SUBLANES)
    buf = _dispatch([s[1] for s in streams], dests, [s[4] for s in streams], pend.astype(jnp.int32),
                    n_blocks * MOE_BLOCK)
    y = _experts(buf, blk_e, n_used, layer, wg, wu, wd)
    return [_combine(x, route, g2, y, dest.reshape(x.shape[0] // ts, 1, 2 * ts), ts)
            for (x, _, route, g2, ts), dest in zip(streams, dests)]


def _block_diag_ones(width):
    idx = np.arange(width) // HALF
    return jnp.asarray((idx[:, None] == idx[None, :]).astype(np.float32), dtype=BF16)


def _rope_tables(n_tok):
    t = jnp.arange(n_tok, dtype=jnp.int32)
    pos = jnp.stack([t // GRID_W, t % GRID_W], axis=-1).astype(F32)
    inv = ROPE_THETA ** (-jnp.arange(ROPE_FREQS, dtype=F32) / ROPE_FREQS)
    ang = pos[:, :, None] * inv
    cos, sin = jnp.cos(ang), jnp.sin(ang)
    c = jnp.concatenate([cos, cos], axis=-1).reshape(n_tok, HEAD_DIM)
    s = jnp.concatenate([-sin, sin], axis=-1).reshape(n_tok, HEAD_DIM)
    return jnp.tile(c, (1, 2)), jnp.tile(s, (1, 2))


def _tile_vec(v, reps):
    return jnp.tile(v.astype(F32), reps).reshape(1, -1)


def kernel(x, c, ctx, c_ctx, router_w, router_b, ada_w, ada_b, norm1_g, norm2_g, ev_w_in, ev_w_out, a_q_norm,
           a_k_norm, b_v_norm, b_ws, b_bs, od_w_in, od_w_out, c_q_norm, c_k_norm, c_rpb, d_dw, d_dw_b, d_ln_g,
           d_ln_b, moe_w_gate, moe_w_up, moe_w_down):
    assert x.shape[0] == 1 and ctx.shape[0] == 1
    s_len, n_ctx = x.shape[1], ctx.shape[1]
    ts_lat = min(512, s_len)
    ts_ctx = n_ctx
    x_lat = x[0]
    x_ctx = ctx[0]

    cc = jnp.zeros((8, D_MODEL), F32).at[0].set(c[0]).at[1].set(c_ctx)
    mods = _modulation(cc, ada_w, ada_b)

    def mod(l, row, k):
        return mods[l, row, k * D_MODEL:(k + 1) * D_MODEL].reshape(1, D_MODEL)

    bd512 = _block_diag_ones(A_Q)
    bd128 = _block_diag_ones(LANES)
    cos_lat, sin_lat = _rope_tables(s_len)
    cos_ctx, sin_ctx = jnp.ones((n_ctx, LANES), F32), jnp.zeros((n_ctx, LANES), F32)
    e_of_row = np.arange(N_EXPERTS)
    row_of_e = (e_of_row % EXPERTS_PER_GROUP) * SUBLANES + e_of_row // EXPERTS_PER_GROUP
    rw = jnp.zeros((LANES, D_MODEL), F32).at[row_of_e].set(router_w.T)
    rwh = rw.astype(BF16)
    rwl = (rw - rwh.astype(F32)).astype(BF16)
    rbc = jnp.zeros((LANES, 1), F32).at[row_of_e, 0].set(router_b)
    router = (rwh, rwl, rbc)

    for l in range(DEPTH):
        need_ctx = l < DEPTH - 1
        i = l // 2
        g_n1 = norm1_g[l].reshape(1, D_MODEL)
        g_n2 = norm2_g[l].reshape(1, D_MODEL)
        m_lat = [mod(l, 0, k) for k in range(6)]
        m_ctx = [mod(l, 1, k) for k in range(6)]
        if l % 2 == 0:
            w_in = ev_w_in[i].astype(BF16)
            w_out = ev_w_out[i].astype(BF16)
            consts = (bd512, bd128, _tile_vec(a_q_norm[i], A_HEADS), _tile_vec(a_k_norm[i], A_KV_HEADS),
                      b_v_norm[i].reshape(1, B_WIDTH).astype(F32), b_ws[i].astype(BF16).reshape(B_GROUPS // 2, 2 * CHUNK, CHUNK),
                      jnp.repeat(b_bs[i].T, HALF, axis=1))
            q_l, kd_l, v_l, b_l = _proj_even(x_lat, (g_n1, m_lat[1], m_lat[0]), w_in, consts, cos_lat, sin_lat, ts_lat)
            q_c, kd_c, v_c, b_c = _proj_even(x_ctx, (g_n1, m_ctx[1], m_ctx[0]), w_in, consts, cos_ctx, sin_ctx, ts_ctx)
            kd_cc = jnp.concatenate([kd_c, kd_c], axis=2)
            v_cc = jnp.concatenate([v_c, jnp.zeros_like(v_c)], axis=1)
            kd_all = jnp.concatenate([kd_l, kd_cc], axis=2)
            v_all = jnp.concatenate([v_l, v_cc], axis=1)
            mix_l = (_attn_even(q_l, kd_all, v_all, min(ATTN_TQ, s_len), ATTN_TK), b_l)
            if need_ctx:
                mix_c = (_attn_even(q_c, kd_cc, v_cc, min(ATTN_TQ, n_ctx), ATTN_TK), b_c)
        else:
            w_in = od_w_in[i].astype(BF16)
            w_out = od_w_out[i].astype(BF16)
            consts = (bd512, _tile_vec(c_q_norm[i], C_HEADS), _tile_vec(c_k_norm[i], C_HEADS))
            q_l, k_l, v_l, y_l = _proj_odd(x_lat, (g_n1, m_lat[1], m_lat[0]), w_in, consts, ts_lat)
            q_c, k_c, v_c, y_c = _proj_odd(x_ctx, (g_n1, m_ctx[1], m_ctx[0]), w_in, consts, ts_ctx)
            dw = jnp.zeros((CONV_W + 1, D_CH), F32).at[:CONV_W].set(d_dw[i])
            conv_p = (dw, d_dw_b[i].reshape(1, D_CH), d_ln_g[i].reshape(1, D_CH), d_ln_b[i].reshape(1, D_CH))
            mix_l = (_na_attention(q_l, k_l, v_l, k_c, v_c, _na_bias(c_rpb[i])), _conv_module(y_l, *conv_p, ts_lat))
            if need_ctx:
                mix_c = (_mha_small(q_c, k_c, v_c), _conv_module(y_c, *conv_p, ts_ctx))
        xn, h2, route, cnt = _post_mix(x_lat, *mix_l, w_out, (m_lat[2], g_n2, m_lat[4], m_lat[3]), router,
                                       jnp.zeros((N_EXPERTS, LANES), F32), ts_lat)
        streams = [(xn, h2, route, m_lat[5], ts_lat)]
        if need_ctx:
            xn, h2, route, cnt = _post_mix(x_ctx, *mix_c, w_out, (m_ctx[2], g_n2, m_ctx[4], m_ctx[3]), router,
                                           cnt, ts_ctx)
            streams.append((xn, h2, route, m_ctx[5], ts_ctx))
        outs = _moe(streams, cnt, l, moe_w_gate, moe_w_up, moe_w_down)
        x_lat = outs[0]
        if need_ctx:
            x_ctx = outs[1]
    return x_lat[None]
```
